```python
import math
import jax, jax.numpy as jnp
from jax import lax
import numpy as np

D_MODEL = 1024
BATCH = 4
SEQ = 4096
DEPTH = 1

CHUNK = 64
N_META = 16
POOL_WIDTH = D_MODEL // 2
POOL_WINDOWS = (2, 4, 8, 16)
POOL_GROUPS = len(POOL_WINDOWS)
POOL_GROUP_DIM = POOL_WIDTH // POOL_GROUPS
N_HEADS = 8
HEAD_DIM = 64
ATTN_WIDTH = N_HEADS * 2 * HEAD_DIM
N_BUCKETS = 32
MAX_DISTANCE = 128
D_FF = ((-(-8 * D_MODEL // 3)) + 255) // 256 * 256
IN_COLS = POOL_WIDTH + 3 * ATTN_WIDTH + 2 * D_MODEL
QBLK = 128
NORM_EPS = 1e-6
NEG_INF = -1e30
BIG_CHUNK = 2 ** 30

kernel_name = "gated_pool_diffattn_hybrid_block"


def rmsnorm(x, w):
    xf = x.astype(jnp.float32)
    var = jnp.mean(xf * xf, axis=-1, keepdims=True)
    return (xf * lax.rsqrt(var + NORM_EPS) * w.astype(jnp.float32)).astype(x.dtype)


def t5_bucket(rel):
    nb = N_BUCKETS // 2
    ret = jnp.where(rel > 0, nb, 0)
    n = jnp.abs(rel)
    max_exact = nb // 2
    nf = jnp.maximum(n, max_exact).astype(jnp.float32)
    large = max_exact + (jnp.log(nf / max_exact) / math.log(MAX_DISTANCE / max_exact)
                         * (nb - max_exact)).astype(jnp.int32)
    large = jnp.minimum(large, nb - 1)
    return ret + jnp.where(n < max_exact, n, large)


def chunk_ids(pos, n_valid):
    cid = jnp.where(pos < N_META, 0, 1 + (pos - N_META) // CHUNK)
    return jnp.where(pos < n_valid, cid, BIG_CHUNK)


def pool_mixer(u, group_w, scale):
    B, L, C = u.shape
    uf = u.astype(jnp.float32)
    cs = jnp.concatenate([jnp.zeros((B, 1, C), jnp.float32), jnp.cumsum(uf, axis=1)], axis=1)
    t = jnp.arange(L)
    outs = []
    for g, w in enumerate(POOL_WINDOWS):
        sl = slice(g * POOL_GROUP_DIM, (g + 1) * POOL_GROUP_DIM)
        csg = cs[..., sl]
        lo = jnp.maximum(t + 1 - w, 0)
        cnt = (t + 1 - lo).astype(jnp.float32)
        mean = (csg[:, 1:] - csg[:, lo]) / cnt[None, :, None]
        outs.append(mean - uf[..., sl])
    pooled = jnp.stack(outs, axis=2).astype(u.dtype)
    mixed = jnp.einsum('blgc,gcd->blgd', pooled, group_w)
    return mixed.reshape(B, L, C) * scale


def diff_attention(q, k, v, bias_table, lam):
    B, L = q.shape[0], q.shape[1]
    Lp = -(-L // QBLK) * QBLK
    pad = Lp - L

    def to_bhld(a):
        a = jnp.pad(a, [(0, 0), (0, pad)] + [(0, 0)] * (a.ndim - 2))
        return jnp.swapaxes(a, 1, 2)

    q1, q2 = to_bhld(q[..., 0, :]), to_bhld(q[..., 1, :])
    k1, k2 = to_bhld(k[..., 0, :]), to_bhld(k[..., 1, :])
    vp = to_bhld(v)
    pos = jnp.arange(Lp)
    cid = chunk_ids(pos, L)
    nblk = Lp // QBLK
    scale = HEAD_DIM ** -0.5

    def blockify(a):
        return a.reshape(B, N_HEADS, nblk, QBLK, a.shape[-1]).transpose(2, 0, 1, 3, 4)

    def one_block(args):
        qa, qb, qpos = args
        rel = pos[None, :] - qpos[:, None]
        bias = jnp.transpose(bias_table[t5_bucket(rel)], (2, 0, 1)).astype(jnp.float32)
        visible = cid[None, :] <= chunk_ids(qpos, L)[:, None]

        def probs(qx, kx):
            s = jnp.einsum('bhqd,bhkd->bhqk', qx, kx).astype(jnp.float32) * scale + bias[None]
            s = jnp.where(visible[None, None], s, NEG_INF)
            return jax.nn.softmax(s, axis=-1)

        w = probs(qa, k1) - lam * probs(qb, k2)
        return jnp.einsum('bhqk,bhkd->bhqd', w.astype(vp.dtype), vp)

    o = lax.map(one_block, (blockify(q1), blockify(q2), pos.reshape(nblk, QBLK)))
    o = o.transpose(1, 0, 3, 2, 4).reshape(B, Lp, N_HEADS, 2 * HEAD_DIM)
    return o[:, :L]


def hybrid_layer(h, layer_idx, bias_table, mix_norm_w, w_in, pool_group_w, pool_scale,
                 lambda_q1, lambda_k1, lambda_q2, lambda_k2, subln_w,
                 w_pool_out, w_attn_out, w_o, ffn_norm_w, w_gate, w_up, w_down):
    B, L, _ = h.shape
    xn = rmsnorm(h, mix_norm_w)
    proj = xn @ w_in
    offs = np.cumsum([POOL_WIDTH, ATTN_WIDTH, ATTN_WIDTH, ATTN_WIDTH, D_MODEL]).tolist()
    u_pool, q, k, v, g_pool, g_attn = jnp.split(proj, offs, axis=-1)

    pool_out = pool_mixer(u_pool, pool_group_w, pool_scale)

    lam_init = 0.8 - 0.6 * math.exp(-0.3 * layer_idx)
    lam = (jnp.exp(jnp.sum(lambda_q1.astype(jnp.float32) * lambda_k1.astype(jnp.float32)))
           - jnp.exp(jnp.sum(lambda_q2.astype(jnp.float32) * lambda_k2.astype(jnp.float32)))
           + lam_init)
    o = diff_attention(q.reshape(B, L, N_HEADS, 2, HEAD_DIM),
                       k.reshape(B, L, N_HEADS, 2, HEAD_DIM),
                       v.reshape(B, L, N_HEADS, 2 * HEAD_DIM), bias_table, lam)
    o = rmsnorm(o, subln_w) * (1.0 - lam_init)
    attn_out = o.reshape(B, L, ATTN_WIDTH)

    merged = (jax.nn.sigmoid(g_pool) * (pool_out @ w_pool_out)
              + jax.nn.sigmoid(g_attn) * (attn_out @ w_attn_out))
    h = h + merged @ w_o

    hn = rmsnorm(h, ffn_norm_w)
    h = h + (jax.nn.silu(hn @ w_gate) * (hn @ w_up)) @ w_down
    return h


def setup_inputs(seed: int = 0) -> dict:
    key = jax.random.key(seed)
    ks = jax.random.split(key, 24)
    f32 = jnp.float32

    def nrm(k, shape, scale):
        return jax.random.normal(k, shape, f32) * scale

    return {
        "x": nrm(ks[0], (BATCH, SEQ, D_MODEL), 1.0),
        "meta_tokens": nrm(ks[1], (N_META, D_MODEL), 1.0),
        "rel_bias_table": nrm(ks[2], (N_BUCKETS, N_HEADS), 0.5),
        "mix_norm_w": 1.0 + nrm(ks[3], (DEPTH, D_MODEL), 0.02),
        "w_in": nrm(ks[4], (DEPTH, D_MODEL, IN_COLS), D_MODEL ** -0.5),
        "pool_group_w": nrm(ks[5], (DEPTH, POOL_GROUPS, POOL_GROUP_DIM, POOL_GROUP_DIM), POOL_GROUP_DIM ** -0.5),
        "pool_scale": 1.0 + nrm(ks[6], (DEPTH, POOL_WIDTH), 0.02),
        "lambda_q1": nrm(ks[7], (DEPTH, HEAD_DIM), 0.1),
        "lambda_k1": nrm(ks[8], (DEPTH, HEAD_DIM), 0.1),
        "lambda_q2": nrm(ks[9], (DEPTH, HEAD_DIM), 0.1),
        "lambda_k2": nrm(ks[10], (DEPTH, HEAD_DIM), 0.1),
        "subln_w": 1.0 + nrm(ks[11], (DEPTH, 2 * HEAD_DIM), 0.02),
        "w_pool_out": nrm(ks[12], (DEPTH, POOL_WIDTH, D_MODEL), POOL_WIDTH ** -0.5),
        "w_attn_out": nrm(ks[13], (DEPTH, ATTN_WIDTH, D_MODEL), ATTN_WIDTH ** -0.5),
        "w_o": nrm(ks[14], (DEPTH, D_MODEL, D_MODEL), D_MODEL ** -0.5),
        "ffn_norm_w": 1.0 + nrm(ks[15], (DEPTH, D_MODEL), 0.02),
        "w_gate": nrm(ks[16], (DEPTH, D_MODEL, D_FF), D_MODEL ** -0.5),
        "w_up": nrm(ks[17], (DEPTH, D_MODEL, D_FF), D_MODEL ** -0.5),
        "w_down": nrm(ks[18], (DEPTH, D_FF, D_MODEL), D_FF ** -0.5),
        "final_norm_w": 1.0 + nrm(ks[19], (D_MODEL,), 0.02),
    }


def reference(x, meta_tokens, rel_bias_table, mix_norm_w, w_in, pool_group_w, pool_scale,
              lambda_q1, lambda_k1, lambda_q2, lambda_k2, subln_w, w_pool_out, w_attn_out,
              w_o, ffn_norm_w, w_gate, w_up, w_down, final_norm_w):
    B = x.shape[0]
    meta = jnp.broadcast_to(meta_tokens[None].astype(x.dtype), (B, N_META, x.shape[-1]))
    h = jnp.concatenate([meta, x], axis=1)
    for i in range(DEPTH):
        h = hybrid_layer(h, i, rel_bias_table, mix_norm_w[i], w_in[i], pool_group_w[i], pool_scale[i],
                         lambda_q1[i], lambda_k1[i], lambda_q2[i], lambda_k2[i], subln_w[i],
                         w_pool_out[i], w_attn_out[i], w_o[i], ffn_norm_w[i],
                         w_gate[i], w_up[i], w_down[i])
    h = rmsnorm(h, final_norm_w)
    return h[:, N_META:]
```

```python
import functools
import math

import jax
import jax.numpy as jnp
import numpy as np
from jax import lax
from jax.experimental import pallas as pl
from jax.experimental.pallas import tpu as pltpu

CHUNK = 64
N_META = 16
POOL_WINDOWS = (2, 4, 8, 16)
N_HEADS = 8
HEAD_DIM = 64
N_BUCKETS = 32
MAX_DISTANCE = 128
NORM_EPS = 1e-6
NEG_INF = -1e30
LAYER_IDX = 0

V_DIM = 2 * HEAD_DIM
LANES = 128
META_PAD = LANES

PROJ_ROWS = 512
ATTN_Q = 256
ATTN_K = 256
MERGE_ROWS = 256
FFN_COLS = 256
VMEM_LIMIT = 56 * 1024 * 1024

BF16 = jnp.bfloat16
F32 = jnp.float32


def _const_spec(shape):
    return pl.BlockSpec(shape, lambda *_: (0,) * len(shape), pipeline_mode=pl.Buffered(1))


def _rms(x, w):
    var = jnp.mean(x * x, axis=-1, keepdims=True)
    return x * lax.rsqrt(var + NORM_EPS) * w


def _in_proj_kernel(x_ref, nw_ref, w_ref, u_ref, q_ref, k_ref, v_ref, gp_ref, ga_ref,
                    *, pool_w, attn_w, d_model):
    xn = _rms(x_ref[...], nw_ref[...]).astype(BF16)

    def proj(lo, width):
        return jnp.dot(xn, w_ref[:, lo:lo + width], preferred_element_type=F32)

    lo = 0
    u_ref[...] = proj(lo, pool_w)
    lo += pool_w
    q_ref[...] = (proj(lo, attn_w) * (HEAD_DIM ** -0.5)).astype(BF16)
    lo += attn_w
    k_ref[...] = proj(lo, attn_w).astype(BF16)
    lo += attn_w
    v_ref[...] = proj(lo, attn_w).astype(BF16)
    lo += attn_w
    gp_ref[...] = jax.nn.sigmoid(proj(lo, d_model))
    lo += d_model
    ga_ref[...] = jax.nn.sigmoid(proj(lo, d_model))


def _in_proj(x2d, norm_w, w_in_bf, *, rows, pool_w, attn_w):
    m, d_model = x2d.shape
    in_cols = w_in_bf.shape[1]
    row_spec = lambda width: pl.BlockSpec((rows, width), lambda i: (i, 0))
    out_shape = (
        jax.ShapeDtypeStruct((m, pool_w), F32),
        jax.ShapeDtypeStruct((m, attn_w), BF16),
        jax.ShapeDtypeStruct((m, attn_w), BF16),
        jax.ShapeDtypeStruct((m, attn_w), BF16),
        jax.ShapeDtypeStruct((m, d_model), F32),
        jax.ShapeDtypeStruct((m, d_model), F32),
    )
    return pl.pallas_call(
        functools.partial(_in_proj_kernel, pool_w=pool_w, attn_w=attn_w, d_model=d_model),
        grid=(m // rows,),
        in_specs=[row_spec(d_model), _const_spec((1, d_model)), _const_spec((d_model, in_cols))],
        out_specs=tuple(row_spec(s.shape[1]) for s in out_shape),
        out_shape=out_shape,
        compiler_params=pltpu.CompilerParams(
            dimension_semantics=("arbitrary",), vmem_limit_bytes=VMEM_LIMIT),
        name="in_proj",
    )(x2d, norm_w.reshape(1, d_model), w_in_bf)


def _attn_kernel(far_ref, q_ref, k_ref, v_ref, km_ref, vm_ref, bd_ref, bm_ref, lam_ref, sw_ref,
                 o_ref, m_sc, l_sc, acc_sc, *, lam_init):
    h = pl.program_id(1)
    i = pl.program_id(2)
    tq = q_ref.shape[0]

    q = q_ref[...]
    lane = lax.broadcasted_iota(jnp.int32, q.shape, 1)
    zero = jnp.zeros_like(q)
    qs = jnp.concatenate([jnp.where(lane < HEAD_DIM, q, zero),
                          jnp.where(lane >= HEAD_DIM, q, zero)], axis=0)

    def scores(k_blk, bias):
        s = lax.dot_general(qs, k_blk, (((1,), (1,)), ((), ())), preferred_element_type=F32)
        if bias is None:
            return s
        return (s.reshape(2, tq, -1) + bias[None]).reshape(s.shape)

    def update(s, v_blk, shift):
        m_prev = m_sc[...]
        m_new = jnp.maximum(m_prev, jnp.max(s, axis=1, keepdims=True) + shift)
        alpha = jnp.exp(m_prev - m_new)
        p = jnp.exp(s - (m_new - shift))
        l_sc[...] = alpha * l_sc[...] + jnp.sum(p, axis=1, keepdims=True)
        acc_sc[...] = alpha * acc_sc[...] + jnp.dot(p.astype(BF16), v_blk,
                                                    preferred_element_type=F32)
        m_sc[...] = m_new

    s = scores(km_ref[...], bm_ref[jnp.minimum(i, 1)])
    m0 = jnp.max(s, axis=1, keepdims=True)
    p = jnp.exp(s - m0)
    m_sc[...] = m0
    l_sc[...] = jnp.sum(p, axis=1, keepdims=True)
    acc_sc[...] = jnp.dot(p.astype(BF16), vm_ref[...], preferred_element_type=F32)

    tk = bd_ref.shape[2]
    far = far_ref[h]

    def far_block(j, carry):
        rows = pl.ds(pl.multiple_of(j * tk, tk), tk)
        update(scores(k_ref[rows, :], None), v_ref[rows, :], far)
        return carry

    lax.fori_loop(0, i - 1, far_block, 0)

    @pl.when(i >= 1)
    def _():
        rows = pl.ds(pl.multiple_of((i - 1) * tk, tk), tk)
        update(scores(k_ref[rows, :], bd_ref[1]), v_ref[rows, :], 0.0)

    rows = pl.ds(pl.multiple_of(i * tk, tk), tk)
    update(scores(k_ref[rows, :], bd_ref[0]), v_ref[rows, :], 0.0)

    lam = (jnp.exp(jnp.sum(lam_ref[0:1, :] * lam_ref[1:2, :], axis=1, keepdims=True))
           - jnp.exp(jnp.sum(lam_ref[2:3, :] * lam_ref[3:4, :], axis=1, keepdims=True))
           + lam_init)
    o = acc_sc[...] / l_sc[...]
    o = o[:tq] - lam * o[tq:]
    o_ref[...] = (_rms(o, sw_ref[...]) * (1.0 - lam_init)).astype(o_ref.dtype)


def _diff_attn(q, k, v, k_meta, v_meta, bias_diag, bias_meta, far_bias, lam_vecs, subln_w,
               *, batch, seq, lam_init):
    nq = seq // ATTN_Q
    qo_spec = pl.BlockSpec((ATTN_Q, V_DIM), lambda b, h, i: (b * nq + i, h))
    kv_spec = pl.BlockSpec((seq, V_DIM), lambda b, h, i: (b, h))
    meta_spec = pl.BlockSpec((META_PAD, V_DIM), lambda b, h, i: (0, h))
    return pl.pallas_call(
        functools.partial(_attn_kernel, lam_init=lam_init),
        grid=(batch, N_HEADS, nq),
        in_specs=[
            pl.BlockSpec(memory_space=pltpu.SMEM),
            qo_spec, kv_spec, kv_spec, meta_spec, meta_spec,
            pl.BlockSpec((None, 2, ATTN_Q, ATTN_K), lambda b, h, i: (h, 0, 0, 0)),
            pl.BlockSpec((None, 2, ATTN_Q, META_PAD), lambda b, h, i: (h, 0, 0, 0)),
            pl.BlockSpec((4, HEAD_DIM), lambda b, h, i: (0, 0)),
            pl.BlockSpec((1, V_DIM), lambda b, h, i: (0, 0)),
        ],
        out_specs=qo_spec,
        out_shape=jax.ShapeDtypeStruct(q.shape, BF16),
        scratch_shapes=[
            pltpu.VMEM((2 * ATTN_Q, 1), F32),
            pltpu.VMEM((2 * ATTN_Q, 1), F32),
            pltpu.VMEM((2 * ATTN_Q, V_DIM), F32),
        ],
        compiler_params=pltpu.CompilerParams(
            dimension_semantics=("arbitrary", "arbitrary", "arbitrary"),
            vmem_limit_bytes=VMEM_LIMIT),
        name="diff_attn",
    )(far_bias, q, k, v, k_meta, v_meta, bias_diag, bias_meta, lam_vecs, subln_w)


def _merge_ffn_kernel(u_ref, uprev_ref, umeta_ref, x_ref, o_ref, gp_ref, ga_ref,
                      gw_ref, ps_ref, wpo_ref, wao_ref, wo_ref, fnw_ref,
                      wg_ref, wu_ref, wd_ref, onw_ref, out_ref, ext_sc, hn_sc, acc_sc,
                      *, tiles_per_seq):
    rows = u_ref.shape[0]
    gdim = gw_ref.shape[1]
    hist = uprev_ref.shape[0]

    first = pl.program_id(0) % tiles_per_seq == 0
    ext_sc[0:hist, :] = jnp.where(first, umeta_ref[...], uprev_ref[...])
    ext_sc[hist:, :] = u_ref[...]

    pool_parts = []
    for g, win in enumerate(POOL_WINDOWS):
        cols = slice(g * gdim, (g + 1) * gdim)
        total = ext_sc[hist:hist + rows, cols]
        for back in range(1, win):
            total = total + ext_sc[hist - back:hist - back + rows, cols]
        pooled = total / float(win) - ext_sc[hist:hist + rows, cols]
        pool_parts.append(jnp.dot(pooled.astype(BF16), gw_ref[g], preferred_element_type=F32))
    pool_out = jnp.concatenate(pool_parts, axis=1) * ps_ref[...]

    merged = (gp_ref[...] * jnp.dot(pool_out.astype(BF16), wpo_ref[...],
                                    preferred_element_type=F32)
              + ga_ref[...] * jnp.dot(o_ref[...], wao_ref[...], preferred_element_type=F32))
    h1 = x_ref[...] + jnp.dot(merged.astype(BF16), wo_ref[...], preferred_element_type=F32)

    hn_sc[...] = _rms(h1, fnw_ref[...]).astype(BF16)
    acc_sc[...] = h1

    def ffn_chunk(c, carry):
        hn = hn_sc[...]
        gate = jnp.dot(hn, wg_ref[c], preferred_element_type=F32)
        up = jnp.dot(hn, wu_ref[c], preferred_element_type=F32)
        act = (jax.nn.silu(gate) * up).astype(BF16)
        acc_sc[...] += jnp.dot(act, wd_ref[c], preferred_element_type=F32)
        return carry

    lax.fori_loop(0, wg_ref.shape[0], ffn_chunk, 0)
    out_ref[...] = _rms(acc_sc[...], onw_ref[...])


def _merge_ffn(u, u_meta, x2d, o, gp, ga, weights, *, seq):
    m, d_model = x2d.shape
    pool_w = u.shape[1]
    hist = u_meta.shape[0]
    rows = MERGE_ROWS
    per_tile = rows // hist
    row_spec = lambda width: pl.BlockSpec((rows, width), lambda i: (i, 0))
    prev_spec = pl.BlockSpec((hist, pool_w), lambda i: (jnp.maximum(i * per_tile - 1, 0), 0))
    return pl.pallas_call(
        functools.partial(_merge_ffn_kernel, tiles_per_seq=seq // rows),
        grid=(m // rows,),
        in_specs=[row_spec(pool_w), prev_spec, _const_spec(u_meta.shape),
                  row_spec(d_model), row_spec(o.shape[1]), row_spec(d_model), row_spec(d_model)]
                 + [_const_spec(w.shape) for w in weights],
        out_specs=row_spec(d_model),
        out_shape=jax.ShapeDtypeStruct((m, d_model), F32),
        scratch_shapes=[pltpu.VMEM((hist + rows, pool_w), F32),
                        pltpu.VMEM((rows, d_model), BF16),
                        pltpu.VMEM((rows, d_model), F32)],
        compiler_params=pltpu.CompilerParams(
            dimension_semantics=("arbitrary",), vmem_limit_bytes=VMEM_LIMIT),
        name="merge_ffn",
    )(u, u, u_meta, x2d, o, gp, ga, *weights)


def _t5_bucket(rel):
    nb = N_BUCKETS // 2
    ret = jnp.where(rel > 0, nb, 0)
    n = jnp.abs(rel)
    max_exact = nb // 2
    nf = jnp.maximum(n, max_exact).astype(jnp.float32)
    large = max_exact + (jnp.log(nf / max_exact) / math.log(MAX_DISTANCE / max_exact)
                         * (nb - max_exact)).astype(jnp.int32)
    large = jnp.minimum(large, nb - 1)
    return ret + jnp.where(n < max_exact, n, large)


def _bias_tables(rel_bias_table):
    tq, tk = ATTN_Q, ATTN_K
    assert tq == tk and tk % CHUNK == 0 and tk + 1 >= MAX_DISTANCE
    span = 2 * tk + N_META
    by_rel = rel_bias_table[_t5_bucket(jnp.arange(-span, span))].astype(F32)
    by_rel = by_rel.T

    row = np.arange(tq)[:, None]
    col = np.arange(tk)[None, :]
    rel_diag = np.stack([col - row, col - row - tk])
    visible = np.stack([(col // CHUNK) <= (row // CHUNK), np.ones((tq, tk), bool)])
    diag = jnp.where(visible[None], by_rel[:, rel_diag + span], NEG_INF)

    mcol = np.arange(META_PAD)[None, :]
    rel_meta = np.minimum(mcol, N_META - 1) - N_META - row
    rel_meta = np.stack([rel_meta, rel_meta - tq])
    meta = jnp.where((mcol < N_META)[None, None], by_rel[:, np.maximum(rel_meta, -span) + span],
                     NEG_INF)
    far = by_rel[:, 0]
    return diag, meta, far


def kernel(x, meta_tokens, rel_bias_table, mix_norm_w, w_in, pool_group_w, pool_scale,
           lambda_q1, lambda_k1, lambda_q2, lambda_k2, subln_w, w_pool_out, w_attn_out,
           w_o, ffn_norm_w, w_gate, w_up, w_down, final_norm_w):
    batch, seq, d_model = x.shape
    assert w_in.shape[0] == 1, "single-layer block"
    pool_w = pool_scale.shape[1]
    attn_w = N_HEADS * V_DIM
    assert seq % ATTN_Q == 0 and seq % MERGE_ROWS == 0 and (batch * seq) % PROJ_ROWS == 0
    assert N_META >= max(POOL_WINDOWS) and MERGE_ROWS % N_META == 0

    w_in_bf = w_in[0].astype(BF16)
    x2d = x.reshape(batch * seq, d_model)
    u, q, k, v, gp, ga = _in_proj(x2d, mix_norm_w[0], w_in_bf, rows=PROJ_ROWS,
                                  pool_w=pool_w, attn_w=attn_w)
    u_meta, _, k_meta, v_meta, _, _ = _in_proj(meta_tokens.astype(x.dtype), mix_norm_w[0], w_in_bf,
                                               rows=N_META, pool_w=pool_w, attn_w=attn_w)
    pad = ((0, META_PAD - N_META), (0, 0))
    k_meta = jnp.pad(k_meta, pad)
    v_meta = jnp.pad(v_meta, pad)

    bias_diag, bias_meta, far_bias = _bias_tables(rel_bias_table)
    lam_init = 0.8 - 0.6 * math.exp(-0.3 * LAYER_IDX)
    lam_vecs = jnp.stack([lambda_q1[0], lambda_k1[0], lambda_q2[0], lambda_k2[0]]).astype(F32)
    o = _diff_attn(q, k, v, k_meta, v_meta, bias_diag, bias_meta, far_bias, lam_vecs,
                   subln_w[0].reshape(1, V_DIM).astype(F32),
                   batch=batch, seq=seq, lam_init=lam_init)

    d_ff = w_gate.shape[2]
    n_chunks = d_ff // FFN_COLS
    assert n_chunks * FFN_COLS == d_ff
    col_chunks = lambda w: w.astype(BF16).reshape(d_model, n_chunks, FFN_COLS).transpose(1, 0, 2)
    weights = (
        pool_group_w[0].astype(BF16), pool_scale[0].reshape(1, pool_w),
        w_pool_out[0].astype(BF16), w_attn_out[0].astype(BF16), w_o[0].astype(BF16),
        ffn_norm_w[0].reshape(1, d_model),
        col_chunks(w_gate[0]), col_chunks(w_up[0]),
        w_down[0].astype(BF16).reshape(n_chunks, FFN_COLS, d_model),
        final_norm_w.reshape(1, d_model))
    out = _merge_ffn(u, u_meta, x2d, o, gp, ga, weights, seq=seq)
    return out.reshape(batch, seq, d_model)
```

```python
import functools
import math

import jax
import jax.numpy as jnp
import numpy as np
from jax import lax
from jax.experimental import pallas as pl
from jax.experimental.pallas import tpu as pltpu

CHUNK = 64
N_META = 16
POOL_WINDOWS = (2, 4, 8, 16)
N_HEADS = 8
HEAD_DIM = 64
N_BUCKETS = 32
MAX_DISTANCE = 128
NORM_EPS = 1e-6
NEG_INF = -1e30
LAYER_IDX = 0

V_DIM = 2 * HEAD_DIM
LANES = 128

PROJ_ROWS = 512
ATTN_Q = 256
ATTN_K = 256
FAR_GROUP = 4
MERGE_ROWS = 256
FFN_COLS = 256
VMEM_LIMIT = 56 * 1024 * 1024

BF16 = jnp.bfloat16
F32 = jnp.float32


def _const_spec(shape):
    return pl.BlockSpec(shape, lambda *_: (0,) * len(shape), pipeline_mode=pl.Buffered(1))


def _rms(x, w):
    var = jnp.mean(x * x, axis=-1, keepdims=True)
    return x * lax.rsqrt(var + NORM_EPS) * w


def _in_proj_kernel(x_ref, nw_ref, w_ref, u_ref, q_ref, k_ref, v_ref, gp_ref, ga_ref,
                    *, pool_w, attn_w, d_model):
    xn = _rms(x_ref[...], nw_ref[...]).astype(BF16)

    def proj(lo, width):
        return jnp.dot(xn, w_ref[:, lo:lo + width], preferred_element_type=F32)

    lo = 0
    u_ref[...] = proj(lo, pool_w)
    lo += pool_w
    q_ref[...] = (proj(lo, attn_w) * (HEAD_DIM ** -0.5)).astype(BF16)
    lo += attn_w
    k_ref[...] = proj(lo, attn_w).astype(BF16)
    lo += attn_w
    v_ref[...] = proj(lo, attn_w).astype(BF16)
    lo += attn_w
    gp_ref[...] = jax.nn.sigmoid(proj(lo, d_model))
    lo += d_model
    ga_ref[...] = jax.nn.sigmoid(proj(lo, d_model))


def _in_proj(x2d, norm_w, w_in_bf, *, rows, pool_w, attn_w):
    m, d_model = x2d.shape
    in_cols = w_in_bf.shape[1]
    row_spec = lambda width: pl.BlockSpec((rows, width), lambda i: (i, 0))
    out_shape = (
        jax.ShapeDtypeStruct((m, pool_w), F32),
        jax.ShapeDtypeStruct((m, attn_w), BF16),
        jax.ShapeDtypeStruct((m, attn_w), BF16),
        jax.ShapeDtypeStruct((m, attn_w), BF16),
        jax.ShapeDtypeStruct((m, d_model), F32),
        jax.ShapeDtypeStruct((m, d_model), F32),
    )
    return pl.pallas_call(
        functools.partial(_in_proj_kernel, pool_w=pool_w, attn_w=attn_w, d_model=d_model),
        grid=(m // rows,),
        in_specs=[row_spec(d_model), _const_spec((1, d_model)), _const_spec((d_model, in_cols))],
        out_specs=tuple(row_spec(s.shape[1]) for s in out_shape),
        out_shape=out_shape,
        compiler_params=pltpu.CompilerParams(
            dimension_semantics=("arbitrary",), vmem_limit_bytes=VMEM_LIMIT),
        name="in_proj",
    )(x2d, norm_w.reshape(1, d_model), w_in_bf)


def _attn_kernel(far_ref, q_ref, k_ref, v_ref, km_ref, vmt_ref, bd_ref, bm_ref, lam_ref, sw_ref,
                 o_ref, vt_sc, m_sc, l_sc, acc_sc, *, lam_init):
    h = pl.program_id(1)
    i = pl.program_id(2)
    tq = q_ref.shape[0]
    n_blk, _, tk = vt_sc.shape

    @pl.when(i == 0)
    def _():
        for blk in range(n_blk):
            vt_sc[blk] = v_ref[blk * tk:(blk + 1) * tk, :].astype(F32).T.astype(BF16)

    qt = q_ref[...].astype(F32).T
    row = lax.broadcasted_iota(jnp.int32, qt.shape, 0)
    zero = jnp.zeros_like(qt)
    qs = jnp.concatenate([jnp.where(row < HEAD_DIM, qt, zero),
                          jnp.where(row >= HEAD_DIM, qt, zero)], axis=1).astype(BF16)

    def scores(k_rows, bias):
        s = jnp.dot(k_rows, qs, preferred_element_type=F32)
        if bias is None:
            return s
        return s + jnp.concatenate([bias, bias], axis=1)

    def update(s, first_blk, n_sub, shift):
        m_prev = m_sc[...]
        m_new = jnp.maximum(m_prev, jnp.max(s, axis=0, keepdims=True) + shift)
        alpha = jnp.exp(m_prev - m_new)
        p = jnp.exp(s - (m_new - shift))
        l_sc[...] = alpha * l_sc[...] + jnp.sum(p, axis=0, keepdims=True)
        p = p.astype(BF16)
        pv = jnp.dot(vt_sc[first_blk], p[0:tk], preferred_element_type=F32)
        for t in range(1, n_sub):
            pv = pv + jnp.dot(vt_sc[first_blk + t], p[t * tk:(t + 1) * tk],
                              preferred_element_type=F32)
        acc_sc[...] = alpha * acc_sc[...] + pv
        m_sc[...] = m_new

    bm = bm_ref[jnp.minimum(i, 1)]
    s = scores(km_ref[...], bm)
    m0 = jnp.max(s, axis=0, keepdims=True)
    p = jnp.exp(s - m0)
    m_sc[...] = m0
    l_sc[...] = jnp.sum(p, axis=0, keepdims=True)
    n_meta = p.shape[0]
    p_pad = jnp.concatenate(
        [p.astype(BF16), jnp.zeros((vmt_ref.shape[1] - n_meta, p.shape[1]), BF16)], axis=0)
    acc_sc[...] = jnp.dot(vmt_ref[...], p_pad, preferred_element_type=F32)

    far = far_ref[h]
    n_far = jnp.maximum(i - 1, 0)
    n_grp = n_far // FAR_GROUP

    def far_group(g, carry):
        j0 = g * FAR_GROUP
        rows = pl.ds(pl.multiple_of(j0 * tk, FAR_GROUP * tk), FAR_GROUP * tk)
        update(scores(k_ref[rows, :], None), j0, FAR_GROUP, far)
        return carry

    lax.fori_loop(0, n_grp, far_group, 0)

    def far_single(j, carry):
        rows = pl.ds(pl.multiple_of(j * tk, tk), tk)
        update(scores(k_ref[rows, :], None), j, 1, far)
        return carry

    lax.fori_loop(n_grp * FAR_GROUP, n_far, far_single, 0)

    @pl.when(i >= 1)
    def _():
        rows = pl.ds(pl.multiple_of((i - 1) * tk, tk), tk)
        update(scores(k_ref[rows, :], bd_ref[1]), i - 1, 1, 0.0)

    rows = pl.ds(pl.multiple_of(i * tk, tk), tk)
    update(scores(k_ref[rows, :], bd_ref[0]), i, 1, 0.0)

    lam = (jnp.exp(jnp.sum(lam_ref[0:1, :] * lam_ref[1:2, :], axis=1, keepdims=True))
           - jnp.exp(jnp.sum(lam_ref[2:3, :] * lam_ref[3:4, :], axis=1, keepdims=True))
           + lam_init)
    o = acc_sc[...] / l_sc[...]
    o = o[:, :tq] - lam * o[:, tq:]
    var = jnp.mean(o * o, axis=0, keepdims=True)
    o = o * lax.rsqrt(var + NORM_EPS) * sw_ref[...] * (1.0 - lam_init)
    o_ref[...] = o.T.astype(o_ref.dtype)


def _diff_attn(q, k, v, k_meta, v_meta_t, bias_diag, bias_meta, far_bias, lam_vecs, subln_w,
               *, batch, seq, lam_init):
    nq = seq // ATTN_Q
    n_meta = k_meta.shape[0]
    qo_spec = pl.BlockSpec((ATTN_Q, V_DIM), lambda b, h, i: (b * nq + i, h))
    kv_spec = pl.BlockSpec((seq, V_DIM), lambda b, h, i: (b, h))
    return pl.pallas_call(
        functools.partial(_attn_kernel, lam_init=lam_init),
        grid=(batch, N_HEADS, nq),
        in_specs=[
            pl.BlockSpec(memory_space=pltpu.SMEM),
            qo_spec, kv_spec, kv_spec,
            pl.BlockSpec((n_meta, V_DIM), lambda b, h, i: (0, h)),
            pl.BlockSpec((V_DIM, LANES), lambda b, h, i: (h, 0)),
            pl.BlockSpec((None, 2, ATTN_K, ATTN_Q), lambda b, h, i: (h, 0, 0, 0)),
            pl.BlockSpec((None, 2, n_meta, ATTN_Q), lambda b, h, i: (h, 0, 0, 0)),
            pl.BlockSpec((4, HEAD_DIM), lambda b, h, i: (0, 0)),
            pl.BlockSpec((V_DIM, 1), lambda b, h, i: (0, 0)),
        ],
        out_specs=qo_spec,
        out_shape=jax.ShapeDtypeStruct(q.shape, BF16),
        scratch_shapes=[
            pltpu.VMEM((seq // ATTN_K, V_DIM, ATTN_K), BF16),
            pltpu.VMEM((1, 2 * ATTN_Q), F32),
            pltpu.VMEM((1, 2 * ATTN_Q), F32),
            pltpu.VMEM((V_DIM, 2 * ATTN_Q), F32),
        ],
        compiler_params=pltpu.CompilerParams(
            dimension_semantics=("arbitrary", "arbitrary", "arbitrary"),
            vmem_limit_bytes=VMEM_LIMIT),
        name="diff_attn",
    )(far_bias, q, k, v, k_meta, v_meta_t, bias_diag, bias_meta, lam_vecs, subln_w)


def _merge_ffn_kernel(u_ref, uprev_ref, umeta_ref, x_ref, o_ref, gp_ref, ga_ref,
                      gw_ref, ps_ref, wpo_ref, wao_ref, wo_ref, fnw_ref,
                      wg_ref, wu_ref, wd_ref, onw_ref, out_ref, ext_sc, hn_sc, acc_sc,
                      *, tiles_per_seq):
    rows = u_ref.shape[0]
    gdim = gw_ref.shape[1]
    hist = uprev_ref.shape[0]

    first = pl.program_id(0) % tiles_per_seq == 0
    ext_sc[0:hist, :] = jnp.where(first, umeta_ref[...], uprev_ref[...])
    ext_sc[hist:, :] = u_ref[...]

    pool_parts = []
    for g, win in enumerate(POOL_WINDOWS):
        cols = slice(g * gdim, (g + 1) * gdim)
        total = ext_sc[hist:hist + rows, cols]
        for back in range(1, win):
            total = total + ext_sc[hist - back:hist - back + rows, cols]
        pooled = total / float(win) - ext_sc[hist:hist + rows, cols]
        pool_parts.append(jnp.dot(pooled.astype(BF16), gw_ref[g], preferred_element_type=F32))
    pool_out = jnp.concatenate(pool_parts, axis=1) * ps_ref[...]

    merged = (gp_ref[...] * jnp.dot(pool_out.astype(BF16), wpo_ref[...],
                                    preferred_element_type=F32)
              + ga_ref[...] * jnp.dot(o_ref[...], wao_ref[...], preferred_element_type=F32))
    h1 = x_ref[...] + jnp.dot(merged.astype(BF16), wo_ref[...], preferred_element_type=F32)

    hn_sc[...] = _rms(h1, fnw_ref[...]).astype(BF16)
    acc_sc[...] = h1

    def ffn_chunk(c, carry):
        hn = hn_sc[...]
        gate = jnp.dot(hn, wg_ref[c], preferred_element_type=F32)
        up = jnp.dot(hn, wu_ref[c], preferred_element_type=F32)
        act = (jax.nn.silu(gate) * up).astype(BF16)
        acc_sc[...] += jnp.dot(act, wd_ref[c], preferred_element_type=F32)
        return carry

    lax.fori_loop(0, wg_ref.shape[0], ffn_chunk, 0)
    out_ref[...] = _rms(acc_sc[...], onw_ref[...])


def _merge_ffn(u, u_meta, x2d, o, gp, ga, weights, *, seq):
    m, d_model = x2d.shape
    pool_w = u.shape[1]
    hist = u_meta.shape[0]
    rows = MERGE_ROWS
    per_tile = rows // hist
    row_spec = lambda width: pl.BlockSpec((rows, width), lambda i: (i, 0))
    prev_spec = pl.BlockSpec((hist, pool_w), lambda i: (jnp.maximum(i * per_tile - 1, 0), 0))
    return pl.pallas_call(
        functools.partial(_merge_ffn_kernel, tiles_per_seq=seq // rows),
        grid=(m // rows,),
        in_specs=[row_spec(pool_w), prev_spec, _const_spec(u_meta.shape),
                  row_spec(d_model), row_spec(o.shape[1]), row_spec(d_model), row_spec(d_model)]
                 + [_const_spec(w.shape) for w in weights],
        out_specs=row_spec(d_model),
        out_shape=jax.ShapeDtypeStruct((m, d_model), F32),
        scratch_shapes=[pltpu.VMEM((hist + rows, pool_w), F32),
                        pltpu.VMEM((rows, d_model), BF16),
                        pltpu.VMEM((rows, d_model), F32)],
        compiler_params=pltpu.CompilerParams(
            dimension_semantics=("arbitrary",), vmem_limit_bytes=VMEM_LIMIT),
        name="merge_ffn",
    )(u, u, u_meta, x2d, o, gp, ga, *weights)


def _t5_bucket(rel):
    nb = N_BUCKETS // 2
    ret = jnp.where(rel > 0, nb, 0)
    n = jnp.abs(rel)
    max_exact = nb // 2
    nf = jnp.maximum(n, max_exact).astype(jnp.float32)
    large = max_exact + (jnp.log(nf / max_exact) / math.log(MAX_DISTANCE / max_exact)
                         * (nb - max_exact)).astype(jnp.int32)
    large = jnp.minimum(large, nb - 1)
    return ret + jnp.where(n < max_exact, n, large)


def _toeplitz(w, n_rows, n_cols):
    period = w.shape[-1]
    flat = jnp.tile(w, (1,) * (w.ndim - 1) + (n_rows,))[..., :n_rows * (period - 1)]
    return flat.reshape(w.shape[:-1] + (n_rows, period - 1))[..., :n_cols]


def _bias_tables(rel_bias_table):
    tq, tk = ATTN_Q, ATTN_K
    assert tq == tk and tk % CHUNK == 0 and tk + 1 >= MAX_DISTANCE
    span = 2 * tk + N_META
    by_rel = rel_bias_table[_t5_bucket(jnp.arange(-span, span))].astype(F32).T
    far = by_rel[:, 0]

    period = 2 * tk
    signed = np.where(np.arange(period) < tk, np.arange(period), np.arange(period) - period)
    diag = jnp.stack([_toeplitz(by_rel[:, -signed - d * tk + span], tk, tq) for d in (0, 1)],
                     axis=1)
    key = np.arange(tk)[:, None]
    qry = np.arange(tq)[None, :]
    visible = np.stack([(key // CHUNK) <= (qry // CHUNK), np.ones((tk, tq), bool)])
    diag = jnp.where(visible[None], diag, NEG_INF)

    near = _toeplitz(by_rel[:, np.maximum(-signed - N_META, -span) + span], N_META, tq)
    meta = jnp.stack([near, jnp.broadcast_to(far[:, None, None], near.shape)], axis=1)
    return diag, meta, far


def kernel(x, meta_tokens, rel_bias_table, mix_norm_w, w_in, pool_group_w, pool_scale,
           lambda_q1, lambda_k1, lambda_q2, lambda_k2, subln_w, w_pool_out, w_attn_out,
           w_o, ffn_norm_w, w_gate, w_up, w_down, final_norm_w):
    batch, seq, d_model = x.shape
    assert w_in.shape[0] == 1, "single-layer block"
    pool_w = pool_scale.shape[1]
    attn_w = N_HEADS * V_DIM
    assert seq % ATTN_Q == 0 and seq % MERGE_ROWS == 0 and (batch * seq) % PROJ_ROWS == 0
    assert N_META >= max(POOL_WINDOWS) and MERGE_ROWS % N_META == 0

    w_in_bf = w_in[0].astype(BF16)
    x2d = x.reshape(batch * seq, d_model)
    u, q, k, v, gp, ga = _in_proj(x2d, mix_norm_w[0], w_in_bf, rows=PROJ_ROWS,
                                  pool_w=pool_w, attn_w=attn_w)
    u_meta, _, k_meta, v_meta, _, _ = _in_proj(meta_tokens.astype(x.dtype), mix_norm_w[0], w_in_bf,
                                               rows=N_META, pool_w=pool_w, attn_w=attn_w)
    v_meta_t = jnp.pad(v_meta.T, ((0, 0), (0, LANES - N_META)))

    bias_diag, bias_meta, far_bias = _bias_tables(rel_bias_table)
    lam_init = 0.8 - 0.6 * math.exp(-0.3 * LAYER_IDX)
    lam_vecs = jnp.stack([lambda_q1[0], lambda_k1[0], lambda_q2[0], lambda_k2[0]]).astype(F32)
    o = _diff_attn(q, k, v, k_meta, v_meta_t, bias_diag, bias_meta, far_bias, lam_vecs,
                   subln_w[0].reshape(V_DIM, 1).astype(F32),
                   batch=batch, seq=seq, lam_init=lam_init)

    d_ff = w_gate.shape[2]
    n_chunks = d_ff // FFN_COLS
    assert n_chunks * FFN_COLS == d_ff
    col_chunks = lambda w: w.astype(BF16).reshape(d_model, n_chunks, FFN_COLS).transpose(1, 0, 2)
    weights = (
        pool_group_w[0].astype(BF16), pool_scale[0].reshape(1, pool_w),
        w_pool_out[0].astype(BF16), w_attn_out[0].astype(BF16), w_o[0].astype(BF16),
        ffn_norm_w[0].reshape(1, d_model),
        col_chunks(w_gate[0]), col_chunks(w_up[0]),
        w_down[0].astype(BF16).reshape(n_chunks, FFN_COLS, d_model),
        final_norm_w.reshape(1, d_model))
    out = _merge_ffn(u, u_meta, x2d, o, gp, ga, weights, seq=seq)
    return out.reshape(batch, seq, d_model)
```

```python
import functools
import math

import jax
import jax.numpy as jnp
import numpy as np
from jax import lax
from jax.experimental import pallas as pl
from jax.experimental.pallas import tpu as pltpu

CHUNK = 64
N_META = 16
POOL_WINDOWS = (2, 4, 8, 16)
N_HEADS = 8
HEAD_DIM = 64
N_BUCKETS = 32
MAX_DISTANCE = 128
NORM_EPS = 1e-6
NEG_INF = -1e30
LAYER_IDX = 0

V_DIM = 2 * HEAD_DIM
LANES = 128

PROJ_ROWS = 512
ATTN_Q = 512
ATTN_K = 512
MERGE_ROWS = 256
FFN_COLS = 256
VMEM_LIMIT = 56 * 1024 * 1024

BF16 = jnp.bfloat16
F32 = jnp.float32


def _const_spec(shape):
    return pl.BlockSpec(shape, lambda *_: (0,) * len(shape), pipeline_mode=pl.Buffered(1))


def _rms(x, w):
    var = jnp.mean(x * x, axis=-1, keepdims=True)
    return x * lax.rsqrt(var + NORM_EPS) * w


def _in_proj_kernel(x_ref, nw_ref, w_ref, u_ref, q_ref, k_ref, v_ref, gp_ref, ga_ref,
                    *, pool_w, attn_w, d_model):
    xn = _rms(x_ref[...], nw_ref[...]).astype(BF16)

    def proj(lo, width):
        return jnp.dot(xn, w_ref[:, lo:lo + width], preferred_element_type=F32)

    lo = 0
    u_ref[...] = proj(lo, pool_w)
    lo += pool_w
    q_ref[...] = (proj(lo, attn_w) * (HEAD_DIM ** -0.5)).astype(BF16)
    lo += attn_w
    k_ref[...] = proj(lo, attn_w).astype(BF16)
    lo += attn_w
    v_ref[...] = proj(lo, attn_w).astype(BF16)
    lo += attn_w
    gp_ref[...] = jax.nn.sigmoid(proj(lo, d_model))
    lo += d_model
    ga_ref[...] = jax.nn.sigmoid(proj(lo, d_model))


def _in_proj(x2d, norm_w, w_in_bf, *, rows, pool_w, attn_w):
    m, d_model = x2d.shape
    in_cols = w_in_bf.shape[1]
    row_spec = lambda width: pl.BlockSpec((rows, width), lambda i: (i, 0))
    out_shape = (
        jax.ShapeDtypeStruct((m, pool_w), F32),
        jax.ShapeDtypeStruct((m, attn_w), BF16),
        jax.ShapeDtypeStruct((m, attn_w), BF16),
        jax.ShapeDtypeStruct((m, attn_w), BF16),
        jax.ShapeDtypeStruct((m, d_model), F32),
        jax.ShapeDtypeStruct((m, d_model), F32),
    )
    return pl.pallas_call(
        functools.partial(_in_proj_kernel, pool_w=pool_w, attn_w=attn_w, d_model=d_model),
        grid=(m // rows,),
        in_specs=[row_spec(d_model), _const_spec((1, d_model)), _const_spec((d_model, in_cols))],
        out_specs=tuple(row_spec(s.shape[1]) for s in out_shape),
        out_shape=out_shape,
        compiler_params=pltpu.CompilerParams(
            dimension_semantics=("arbitrary",), vmem_limit_bytes=VMEM_LIMIT),
        name="in_proj",
    )(x2d, norm_w.reshape(1, d_model), w_in_bf)


def _attn_kernel(far_ref, q_ref, k_ref, v_ref, km_ref, vmt_ref, bd_ref, bm_ref, lam_ref, sw_ref,
                 o_ref, vt_sc, qs_sc, s_sc, mx_sc, m_sc, l_sc, acc_sc, *, lam_init):
    h = pl.program_id(1)
    i = pl.program_id(2)
    tq = q_ref.shape[0]
    n_blk, _, tk = vt_sc.shape

    @pl.when(i == 0)
    def _():
        for blk in range(n_blk):
            vt_sc[blk] = v_ref[blk * tk:(blk + 1) * tk, :].astype(F32).T.astype(BF16)

    qt = q_ref[...].astype(F32).T
    row = lax.broadcasted_iota(jnp.int32, qt.shape, 0)
    zero = jnp.zeros_like(qt)
    qs_sc[...] = jnp.concatenate([jnp.where(row < HEAD_DIM, qt, zero),
                                  jnp.where(row >= HEAD_DIM, qt, zero)], axis=1).astype(BF16)

    bm = bm_ref[jnp.minimum(i, 1)]
    s = jnp.dot(km_ref[...], qs_sc[...], preferred_element_type=F32)
    s = s + jnp.concatenate([bm, bm], axis=1)
    m0 = jnp.max(s, axis=0, keepdims=True)
    p = jnp.exp(s - m0)
    m_sc[...] = m0
    l_sc[...] = jnp.sum(p, axis=0, keepdims=True)
    n_meta = p.shape[0]
    p_pad = jnp.concatenate(
        [p.astype(BF16), jnp.zeros((vmt_ref.shape[1] - n_meta, p.shape[1]), BF16)], axis=0)
    acc_sc[...] = jnp.dot(vmt_ref[...], p_pad, preferred_element_type=F32)

    far = far_ref[h]

    def unit_a(r, slot, bias_idx):
        rows = pl.ds(pl.multiple_of((i - r) * tk, tk), tk)
        s = jnp.dot(k_ref[rows, :], qs_sc[...], preferred_element_type=F32)
        if bias_idx is None:
            shift = far
        else:
            bias = bd_ref[bias_idx]
            s = s + jnp.concatenate([bias, bias], axis=1)
            shift = 0.0
        s_sc[slot] = s
        mx_sc[slot] = jnp.max(s, axis=0, keepdims=True) + shift

    def unit_b(r, slot, shift):
        m_prev = m_sc[...]
        m_new = jnp.maximum(m_prev, mx_sc[slot])
        alpha = jnp.exp(m_prev - m_new)
        p = jnp.exp(s_sc[slot] - (m_new - shift))
        l_sc[...] = alpha * l_sc[...] + jnp.sum(p, axis=0, keepdims=True)
        acc_sc[...] = alpha * acc_sc[...] + jnp.dot(vt_sc[i - r], p.astype(BF16),
                                                    preferred_element_type=F32)
        m_sc[...] = m_new

    odd = i % 2 == 1

    @pl.when(jnp.logical_and(i >= 2, odd))
    def _():
        unit_a(i, 1, None)
        unit_a(i - 1, 0, None)
        unit_b(i, 1, far)

    @pl.when(jnp.logical_and(i >= 2, jnp.logical_not(odd)))
    def _():
        unit_a(i, 0, None)

    r_even = i - i % 2

    def pair(t, carry):
        r = r_even - 2 * t
        unit_a(r - 1, 1, None)
        unit_b(r, 0, far)
        unit_a(r - 2, 0, None)
        unit_b(r - 1, 1, far)
        return carry

    lax.fori_loop(0, (i - 2) // 2, pair, 0)

    @pl.when(i >= 2)
    def _():
        unit_a(1, 1, 1)
        unit_b(2, 0, far)
        unit_a(0, 0, 0)
        unit_b(1, 1, 0.0)

    @pl.when(i == 1)
    def _():
        unit_a(1, 1, 1)
        unit_a(0, 0, 0)
        unit_b(1, 1, 0.0)

    @pl.when(i == 0)
    def _():
        unit_a(0, 0, 0)

    unit_b(0, 0, 0.0)

    lam = (jnp.exp(jnp.sum(lam_ref[0:1, :] * lam_ref[1:2, :], axis=1, keepdims=True))
           - jnp.exp(jnp.sum(lam_ref[2:3, :] * lam_ref[3:4, :], axis=1, keepdims=True))
           + lam_init)
    o = acc_sc[...] / l_sc[...]
    o = o[:, :tq] - lam * o[:, tq:]
    var = jnp.mean(o * o, axis=0, keepdims=True)
    o = o * lax.rsqrt(var + NORM_EPS) * sw_ref[...] * (1.0 - lam_init)
    o_ref[...] = o.T.astype(o_ref.dtype)


def _diff_attn(q, k, v, k_meta, v_meta_t, bias_diag, bias_meta, far_bias, lam_vecs, subln_w,
               *, batch, seq, lam_init):
    nq = seq // ATTN_Q
    n_meta = k_meta.shape[0]
    qo_spec = pl.BlockSpec((ATTN_Q, V_DIM), lambda b, h, i: (b * nq + i, h))
    kv_spec = pl.BlockSpec((seq, V_DIM), lambda b, h, i: (b, h))
    return pl.pallas_call(
        functools.partial(_attn_kernel, lam_init=lam_init),
        grid=(batch, N_HEADS, nq),
        in_specs=[
            pl.BlockSpec(memory_space=pltpu.SMEM),
            qo_spec, kv_spec, kv_spec,
            pl.BlockSpec((n_meta, V_DIM), lambda b, h, i: (0, h)),
            pl.BlockSpec((V_DIM, LANES), lambda b, h, i: (h, 0)),
            pl.BlockSpec((None, 2, ATTN_K, ATTN_Q), lambda b, h, i: (h, 0, 0, 0)),
            pl.BlockSpec((None, 2, n_meta, ATTN_Q), lambda b, h, i: (h, 0, 0, 0)),
            pl.BlockSpec((4, HEAD_DIM), lambda b, h, i: (0, 0)),
            pl.BlockSpec((V_DIM, 1), lambda b, h, i: (0, 0)),
        ],
        out_specs=qo_spec,
        out_shape=jax.ShapeDtypeStruct(q.shape, BF16),
        scratch_shapes=[
            pltpu.VMEM((seq // ATTN_K, V_DIM, ATTN_K), BF16),
            pltpu.VMEM((V_DIM, 2 * ATTN_Q), BF16),
            pltpu.VMEM((2, ATTN_K, 2 * ATTN_Q), F32),
            pltpu.VMEM((2, 1, 2 * ATTN_Q), F32),
            pltpu.VMEM((1, 2 * ATTN_Q), F32),
            pltpu.VMEM((1, 2 * ATTN_Q), F32),
            pltpu.VMEM((V_DIM, 2 * ATTN_Q), F32),
        ],
        compiler_params=pltpu.CompilerParams(
            dimension_semantics=("arbitrary", "arbitrary", "arbitrary"),
            vmem_limit_bytes=VMEM_LIMIT),
        name="diff_attn",
    )(far_bias, q, k, v, k_meta, v_meta_t, bias_diag, bias_meta, lam_vecs, subln_w)


def _merge_ffn_kernel(u_ref, uprev_ref, umeta_ref, x_ref, o_ref, gp_ref, ga_ref,
                      gw_ref, ps_ref, wpo_ref, wao_ref, wo_ref, fnw_ref,
                      wg_ref, wu_ref, wd_ref, onw_ref, out_ref, ext_sc, hn_sc, acc_sc,
                      *, tiles_per_seq):
    rows = u_ref.shape[0]
    gdim = gw_ref.shape[1]
    hist = uprev_ref.shape[0]

    first = pl.program_id(0) % tiles_per_seq == 0
    ext_sc[0:hist, :] = jnp.where(first, umeta_ref[...], uprev_ref[...])
    ext_sc[hist:, :] = u_ref[...]

    pool_parts = []
    for g, win in enumerate(POOL_WINDOWS):
        cols = slice(g * gdim, (g + 1) * gdim)
        total = ext_sc[hist:hist + rows, cols]
        for back in range(1, win):
            total = total + ext_sc[hist - back:hist - back + rows, cols]
        pooled = total / float(win) - ext_sc[hist:hist + rows, cols]
        pool_parts.append(jnp.dot(pooled.astype(BF16), gw_ref[g], preferred_element_type=F32))
    pool_out = jnp.concatenate(pool_parts, axis=1) * ps_ref[...]

    merged = (gp_ref[...] * jnp.dot(pool_out.astype(BF16), wpo_ref[...],
                                    preferred_element_type=F32)
              + ga_ref[...] * jnp.dot(o_ref[...], wao_ref[...], preferred_element_type=F32))
    h1 = x_ref[...] + jnp.dot(merged.astype(BF16), wo_ref[...], preferred_element_type=F32)

    hn_sc[...] = _rms(h1, fnw_ref[...]).astype(BF16)
    acc_sc[...] = h1

    def ffn_chunk(c, carry):
        hn = hn_sc[...]
        gate = jnp.dot(hn, wg_ref[c], preferred_element_type=F32)
        up = jnp.dot(hn, wu_ref[c], preferred_element_type=F32)
        act = (jax.nn.silu(gate) * up).astype(BF16)
        acc_sc[...] += jnp.dot(act, wd_ref[c], preferred_element_type=F32)
        return carry

    lax.fori_loop(0, wg_ref.shape[0], ffn_chunk, 0)
    out_ref[...] = _rms(acc_sc[...], onw_ref[...])


def _merge_ffn(u, u_meta, x2d, o, gp, ga, weights, *, seq):
    m, d_model = x2d.shape
    pool_w = u.shape[1]
    hist = u_meta.shape[0]
    rows = MERGE_ROWS
    per_tile = rows // hist
    row_spec = lambda width: pl.BlockSpec((rows, width), lambda i: (i, 0))
    prev_spec = pl.BlockSpec((hist, pool_w), lambda i: (jnp.maximum(i * per_tile - 1, 0), 0))
    return pl.pallas_call(
        functools.partial(_merge_ffn_kernel, tiles_per_seq=seq // rows),
        grid=(m // rows,),
        in_specs=[row_spec(pool_w), prev_spec, _const_spec(u_meta.shape),
                  row_spec(d_model), row_spec(o.shape[1]), row_spec(d_model), row_spec(d_model)]
                 + [_const_spec(w.shape) for w in weights],
        out_specs=row_spec(d_model),
        out_shape=jax.ShapeDtypeStruct((m, d_model), F32),
        scratch_shapes=[pltpu.VMEM((hist + rows, pool_w), F32),
                        pltpu.VMEM((rows, d_model), BF16),
                        pltpu.VMEM((rows, d_model), F32)],
        compiler_params=pltpu.CompilerParams(
            dimension_semantics=("arbitrary",), vmem_limit_bytes=VMEM_LIMIT),
        name="merge_ffn",
    )(u, u, u_meta, x2d, o, gp, ga, *weights)


def _t5_bucket(rel):
    nb = N_BUCKETS // 2
    ret = jnp.where(rel > 0, nb, 0)
    n = jnp.abs(rel)
    max_exact = nb // 2
    nf = jnp.maximum(n, max_exact).astype(jnp.float32)
    large = max_exact + (jnp.log(nf / max_exact) / math.log(MAX_DISTANCE / max_exact)
                         * (nb - max_exact)).astype(jnp.int32)
    large = jnp.minimum(large, nb - 1)
    return ret + jnp.where(n < max_exact, n, large)


def _toeplitz(w, n_rows, n_cols):
    period = w.shape[-1]
    flat = jnp.tile(w, (1,) * (w.ndim - 1) + (n_rows,))[..., :n_rows * (period - 1)]
    return flat.reshape(w.shape[:-1] + (n_rows, period - 1))[..., :n_cols]


def _bias_tables(rel_bias_table):
    tq, tk = ATTN_Q, ATTN_K
    assert tq == tk and tk % CHUNK == 0 and tk + 1 >= MAX_DISTANCE
    span = 2 * tk + N_META
    by_rel = rel_bias_table[_t5_bucket(jnp.arange(-span, span))].astype(F32).T
    far = by_rel[:, 0]

    period = 2 * tk
    signed = np.where(np.arange(period) < tk, np.arange(period), np.arange(period) - period)
    diag = jnp.stack([_toeplitz(by_rel[:, -signed - d * tk + span], tk, tq) for d in (0, 1)],
                     axis=1)
    key = np.arange(tk)[:, None]
    qry = np.arange(tq)[None, :]
    visible = np.stack([(key // CHUNK) <= (qry // CHUNK), np.ones((tk, tq), bool)])
    diag = jnp.where(visible[None], diag, NEG_INF)

    near = _toeplitz(by_rel[:, np.maximum(-signed - N_META, -span) + span], N_META, tq)
    meta = jnp.stack([near, jnp.broadcast_to(far[:, None, None], near.shape)], axis=1)
    return diag, meta, far


def kernel(x, meta_tokens, rel_bias_table, mix_norm_w, w_in, pool_group_w, pool_scale,
           lambda_q1, lambda_k1, lambda_q2, lambda_k2, subln_w, w_pool_out, w_attn_out,
           w_o, ffn_norm_w, w_gate, w_up, w_down, final_norm_w):
    batch, seq, d_model = x.shape
    assert w_in.shape[0] == 1, "single-layer block"
    pool_w = pool_scale.shape[1]
    attn_w = N_HEADS * V_DIM
    assert seq % ATTN_Q == 0 and seq % MERGE_ROWS == 0 and (batch * seq) % PROJ_ROWS == 0
    assert N_META >= max(POOL_WINDOWS) and MERGE_ROWS % N_META == 0

    w_in_bf = w_in[0].astype(BF16)
    x2d = x.reshape(batch * seq, d_model)
    u, q, k, v, gp, ga = _in_proj(x2d, mix_norm_w[0], w_in_bf, rows=PROJ_ROWS,
                                  pool_w=pool_w, attn_w=attn_w)
    u_meta, _, k_meta, v_meta, _, _ = _in_proj(meta_tokens.astype(x.dtype), mix_norm_w[0], w_in_bf,
                                               rows=N_META, pool_w=pool_w, attn_w=attn_w)
    v_meta_t = jnp.pad(v_meta.T, ((0, 0), (0, LANES - N_META)))

    bias_diag, bias_meta, far_bias = _bias_tables(rel_bias_table)
    lam_init = 0.8 - 0.6 * math.exp(-0.3 * LAYER_IDX)
    lam_vecs = jnp.stack([lambda_q1[0], lambda_k1[0], lambda_q2[0], lambda_k2[0]]).astype(F32)
    o = _diff_attn(q, k, v, k_meta, v_meta_t, bias_diag, bias_meta, far_bias, lam_vecs,
                   subln_w[0].reshape(V_DIM, 1).astype(F32),
                   batch=batch, seq=seq, lam_init=lam_init)

    d_ff = w_gate.shape[2]
    n_chunks = d_ff // FFN_COLS
    assert n_chunks * FFN_COLS == d_ff
    col_chunks = lambda w: w.astype(BF16).reshape(d_model, n_chunks, FFN_COLS).transpose(1, 0, 2)
    weights = (
        pool_group_w[0].astype(BF16), pool_scale[0].reshape(1, pool_w),
        w_pool_out[0].astype(BF16), w_attn_out[0].astype(BF16), w_o[0].astype(BF16),
        ffn_norm_w[0].reshape(1, d_model),
        col_chunks(w_gate[0]), col_chunks(w_up[0]),
        w_down[0].astype(BF16).reshape(n_chunks, FFN_COLS, d_model),
        final_norm_w.reshape(1, d_model))
    out = _merge_ffn(u, u_meta, x2d, o, gp, ga, weights, seq=seq)
    return out.reshape(batch, seq, d_model)
```

```python
import functools
import math

import jax
import jax.numpy as jnp
import numpy as np
from jax import lax
from jax.experimental import pallas as pl
from jax.experimental.pallas import tpu as pltpu

CHUNK = 64
N_META = 16
POOL_WINDOWS = (2, 4, 8, 16)
N_HEADS = 8
HEAD_DIM = 64
N_BUCKETS = 32
MAX_DISTANCE = 128
NORM_EPS = 1e-6
NEG_INF = -1e30
LAYER_IDX = 0

V_DIM = 2 * HEAD_DIM
LANES = 128

PROJ_ROWS = 512
ATTN_Q = 512
ATTN_K = 512
MERGE_ROWS = 512
FFN_COLS = 256
VMEM_LIMIT = 56 * 1024 * 1024

BF16 = jnp.bfloat16
F32 = jnp.float32


def _const_spec(shape):
    return pl.BlockSpec(shape, lambda *_: (0,) * len(shape), pipeline_mode=pl.Buffered(1))


def _rms(x, w):
    var = jnp.mean(x * x, axis=-1, keepdims=True)
    return x * lax.rsqrt(var + NORM_EPS) * w


def _in_proj_kernel(x_ref, nw_ref, w_ref, u_ref, q_ref, k_ref, v_ref, gp_ref, ga_ref,
                    *, pool_w, attn_w, d_model):
    xn = _rms(x_ref[...], nw_ref[...]).astype(BF16)

    def proj(lo, width):
        return jnp.dot(xn, w_ref[:, lo:lo + width], preferred_element_type=F32)

    lo = 0
    u_ref[...] = proj(lo, pool_w)
    lo += pool_w
    q_ref[...] = (proj(lo, attn_w) * (HEAD_DIM ** -0.5)).astype(BF16)
    lo += attn_w
    k_ref[...] = proj(lo, attn_w).astype(BF16)
    lo += attn_w
    v_ref[...] = proj(lo, attn_w).astype(BF16)
    lo += attn_w
    gp_ref[...] = jax.nn.sigmoid(proj(lo, d_model))
    lo += d_model
    ga_ref[...] = jax.nn.sigmoid(proj(lo, d_model))


def _in_proj(x2d, norm_w, w_in_bf, *, rows, pool_w, attn_w):
    m, d_model = x2d.shape
    in_cols = w_in_bf.shape[1]
    row_spec = lambda width: pl.BlockSpec((rows, width), lambda i: (i, 0))
    out_shape = (
        jax.ShapeDtypeStruct((m, pool_w), F32),
        jax.ShapeDtypeStruct((m, attn_w), BF16),
        jax.ShapeDtypeStruct((m, attn_w), BF16),
        jax.ShapeDtypeStruct((m, attn_w), BF16),
        jax.ShapeDtypeStruct((m, d_model), F32),
        jax.ShapeDtypeStruct((m, d_model), F32),
    )
    return pl.pallas_call(
        functools.partial(_in_proj_kernel, pool_w=pool_w, attn_w=attn_w, d_model=d_model),
        grid=(m // rows,),
        in_specs=[row_spec(d_model), _const_spec((1, d_model)), _const_spec((d_model, in_cols))],
        out_specs=tuple(row_spec(s.shape[1]) for s in out_shape),
        out_shape=out_shape,
        compiler_params=pltpu.CompilerParams(
            dimension_semantics=("arbitrary",), vmem_limit_bytes=VMEM_LIMIT),
        name="in_proj",
    )(x2d, norm_w.reshape(1, d_model), w_in_bf)


def _attn_kernel(far_ref, q_ref, k_ref, v_ref, km_ref, vmt_ref, bd_ref, bm_ref, lam_ref, sw_ref,
                 o_ref, vt_sc, qs_sc, s_sc, mx_sc, m_sc, l_sc, acc_sc, *, lam_init):
    h = pl.program_id(1)
    i = pl.program_id(2)
    tq = q_ref.shape[0]
    n_blk, _, tk = vt_sc.shape

    @pl.when(i == 0)
    def _():
        for blk in range(n_blk):
            vt_sc[blk] = v_ref[blk * tk:(blk + 1) * tk, :].astype(F32).T.astype(BF16)

    qt = q_ref[...].astype(F32).T
    row = lax.broadcasted_iota(jnp.int32, qt.shape, 0)
    zero = jnp.zeros_like(qt)
    qs_sc[...] = jnp.concatenate([jnp.where(row < HEAD_DIM, qt, zero),
                                  jnp.where(row >= HEAD_DIM, qt, zero)], axis=1).astype(BF16)

    bm = bm_ref[jnp.minimum(i, 1)]
    s = jnp.dot(km_ref[...], qs_sc[...], preferred_element_type=F32)
    s = s + jnp.concatenate([bm, bm], axis=1)
    m0 = jnp.max(s, axis=0, keepdims=True)
    p = jnp.exp(s - m0)
    m_sc[...] = m0
    l_sc[...] = jnp.sum(p, axis=0, keepdims=True)
    n_meta = p.shape[0]
    p_pad = jnp.concatenate(
        [p.astype(BF16), jnp.zeros((vmt_ref.shape[1] - n_meta, p.shape[1]), BF16)], axis=0)
    acc_sc[...] = jnp.dot(vmt_ref[...], p_pad, preferred_element_type=F32)

    far = far_ref[h]

    def unit_a(r, slot, bias_idx):
        rows = pl.ds(pl.multiple_of((i - r) * tk, tk), tk)
        s = jnp.dot(k_ref[rows, :], qs_sc[...], preferred_element_type=F32)
        if bias_idx is None:
            shift = far
        else:
            bias = bd_ref[bias_idx]
            s = s + jnp.concatenate([bias, bias], axis=1)
            shift = 0.0
        s_sc[slot] = s
        mx_sc[slot] = jnp.max(s, axis=0, keepdims=True) + shift

    def unit_b(r, slot, shift):
        m_prev = m_sc[...]
        m_new = jnp.maximum(m_prev, mx_sc[slot])
        alpha = jnp.exp(m_prev - m_new)
        p = jnp.exp(s_sc[slot] - (m_new - shift))
        l_sc[...] = alpha * l_sc[...] + jnp.sum(p, axis=0, keepdims=True)
        acc_sc[...] = alpha * acc_sc[...] + jnp.dot(vt_sc[i - r], p.astype(BF16),
                                                    preferred_element_type=F32)
        m_sc[...] = m_new

    odd = i % 2 == 1

    @pl.when(jnp.logical_and(i >= 2, odd))
    def _():
        unit_a(i, 1, None)
        unit_a(i - 1, 0, None)
        unit_b(i, 1, far)

    @pl.when(jnp.logical_and(i >= 2, jnp.logical_not(odd)))
    def _():
        unit_a(i, 0, None)

    r_even = i - i % 2

    def pair(t, carry):
        r = r_even - 2 * t
        unit_a(r - 1, 1, None)
        unit_b(r, 0, far)
        unit_a(r - 2, 0, None)
        unit_b(r - 1, 1, far)
        return carry

    lax.fori_loop(0, (i - 2) // 2, pair, 0)

    @pl.when(i >= 2)
    def _():
        unit_a(1, 1, 1)
        unit_b(2, 0, far)
        unit_a(0, 0, 0)
        unit_b(1, 1, 0.0)

    @pl.when(i == 1)
    def _():
        unit_a(1, 1, 1)
        unit_a(0, 0, 0)
        unit_b(1, 1, 0.0)

    @pl.when(i == 0)
    def _():
        unit_a(0, 0, 0)

    unit_b(0, 0, 0.0)

    lam = (jnp.exp(jnp.sum(lam_ref[0:1, :] * lam_ref[1:2, :], axis=1, keepdims=True))
           - jnp.exp(jnp.sum(lam_ref[2:3, :] * lam_ref[3:4, :], axis=1, keepdims=True))
           + lam_init)
    o = acc_sc[...] / l_sc[...]
    o = o[:, :tq] - lam * o[:, tq:]
    var = jnp.mean(o * o, axis=0, keepdims=True)
    o = o * lax.rsqrt(var + NORM_EPS) * sw_ref[...] * (1.0 - lam_init)
    o_ref[...] = o.T.astype(o_ref.dtype)


def _diff_attn(q, k, v, k_meta, v_meta_t, bias_diag, bias_meta, far_bias, lam_vecs, subln_w,
               *, batch, seq, lam_init):
    nq = seq // ATTN_Q
    n_meta = k_meta.shape[0]
    qo_spec = pl.BlockSpec((ATTN_Q, V_DIM), lambda b, h, i: (b * nq + i, h))
    kv_spec = pl.BlockSpec((seq, V_DIM), lambda b, h, i: (b, h))
    return pl.pallas_call(
        functools.partial(_attn_kernel, lam_init=lam_init),
        grid=(batch, N_HEADS, nq),
        in_specs=[
            pl.BlockSpec(memory_space=pltpu.SMEM),
            qo_spec, kv_spec, kv_spec,
            pl.BlockSpec((n_meta, V_DIM), lambda b, h, i: (0, h)),
            pl.BlockSpec((V_DIM, LANES), lambda b, h, i: (h, 0)),
            pl.BlockSpec((None, 2, ATTN_K, ATTN_Q), lambda b, h, i: (h, 0, 0, 0)),
            pl.BlockSpec((None, 2, n_meta, ATTN_Q), lambda b, h, i: (h, 0, 0, 0)),
            pl.BlockSpec((4, HEAD_DIM), lambda b, h, i: (0, 0)),
            pl.BlockSpec((V_DIM, 1), lambda b, h, i: (0, 0)),
        ],
        out_specs=qo_spec,
        out_shape=jax.ShapeDtypeStruct(q.shape, BF16),
        scratch_shapes=[
            pltpu.VMEM((seq // ATTN_K, V_DIM, ATTN_K), BF16),
            pltpu.VMEM((V_DIM, 2 * ATTN_Q), BF16),
            pltpu.VMEM((2, ATTN_K, 2 * ATTN_Q), F32),
            pltpu.VMEM((2, 1, 2 * ATTN_Q), F32),
            pltpu.VMEM((1, 2 * ATTN_Q), F32),
            pltpu.VMEM((1, 2 * ATTN_Q), F32),
            pltpu.VMEM((V_DIM, 2 * ATTN_Q), F32),
        ],
        compiler_params=pltpu.CompilerParams(
            dimension_semantics=("arbitrary", "arbitrary", "arbitrary"),
            vmem_limit_bytes=VMEM_LIMIT),
        name="diff_attn",
    )(far_bias, q, k, v, k_meta, v_meta_t, bias_diag, bias_meta, lam_vecs, subln_w)


def _merge_ffn_kernel(u_ref, uprev_ref, umeta_ref, x_ref, o_ref, gp_ref, ga_ref,
                      gw_ref, ps_ref, wpo_ref, wao_ref, wo_ref, fnw_ref,
                      wg_ref, wu_ref, wd_ref, onw_ref, out_ref, ext_sc, hn_sc, acc_sc,
                      *, tiles_per_seq):
    rows = u_ref.shape[0]
    gdim = gw_ref.shape[1]
    hist = uprev_ref.shape[0]

    first = pl.program_id(0) % tiles_per_seq == 0
    ext_sc[0:hist, :] = jnp.where(first, umeta_ref[...], uprev_ref[...])
    ext_sc[hist:, :] = u_ref[...]

    pool_parts = []
    for g, win in enumerate(POOL_WINDOWS):
        cols = slice(g * gdim, (g + 1) * gdim)
        total = ext_sc[hist:hist + rows, cols]
        for back in range(1, win):
            total = total + ext_sc[hist - back:hist - back + rows, cols]
        pooled = total / float(win) - ext_sc[hist:hist + rows, cols]
        pool_parts.append(jnp.dot(pooled.astype(BF16), gw_ref[g], preferred_element_type=F32))
    pool_out = jnp.concatenate(pool_parts, axis=1) * ps_ref[...]

    merged = (gp_ref[...] * jnp.dot(pool_out.astype(BF16), wpo_ref[...],
                                    preferred_element_type=F32)
              + ga_ref[...] * jnp.dot(o_ref[...], wao_ref[...], preferred_element_type=F32))
    h1 = x_ref[...] + jnp.dot(merged.astype(BF16), wo_ref[...], preferred_element_type=F32)

    hn_sc[...] = _rms(h1, fnw_ref[...]).astype(BF16)
    acc_sc[...] = h1

    def ffn_chunk(c, carry):
        hn = hn_sc[...]
        gate = jnp.dot(hn, wg_ref[c], preferred_element_type=F32)
        up = jnp.dot(hn, wu_ref[c], preferred_element_type=F32)
        act = (jax.nn.silu(gate) * up).astype(BF16)
        acc_sc[...] += jnp.dot(act, wd_ref[c], preferred_element_type=F32)
        return carry

    lax.fori_loop(0, wg_ref.shape[0], ffn_chunk, 0, unroll=True)
    out_ref[...] = _rms(acc_sc[...], onw_ref[...])


def _merge_ffn(u, u_meta, x2d, o, gp, ga, weights, *, seq):
    m, d_model = x2d.shape
    pool_w = u.shape[1]
    hist = u_meta.shape[0]
    rows = MERGE_ROWS
    per_tile = rows // hist
    row_spec = lambda width: pl.BlockSpec((rows, width), lambda i: (i, 0))
    prev_spec = pl.BlockSpec((hist, pool_w), lambda i: (jnp.maximum(i * per_tile - 1, 0), 0))
    return pl.pallas_call(
        functools.partial(_merge_ffn_kernel, tiles_per_seq=seq // rows),
        grid=(m // rows,),
        in_specs=[row_spec(pool_w), prev_spec, _const_spec(u_meta.shape),
                  row_spec(d_model), row_spec(o.shape[1]), row_spec(d_model), row_spec(d_model)]
                 + [_const_spec(w.shape) for w in weights],
        out_specs=row_spec(d_model),
        out_shape=jax.ShapeDtypeStruct((m, d_model), F32),
        scratch_shapes=[pltpu.VMEM((hist + rows, pool_w), F32),
                        pltpu.VMEM((rows, d_model), BF16),
                        pltpu.VMEM((rows, d_model), F32)],
        compiler_params=pltpu.CompilerParams(
            dimension_semantics=("arbitrary",), vmem_limit_bytes=VMEM_LIMIT),
        name="merge_ffn",
    )(u, u, u_meta, x2d, o, gp, ga, *weights)


def _t5_bucket(rel):
    nb = N_BUCKETS // 2
    ret = jnp.where(rel > 0, nb, 0)
    n = jnp.abs(rel)
    max_exact = nb // 2
    nf = jnp.maximum(n, max_exact).astype(jnp.float32)
    large = max_exact + (jnp.log(nf / max_exact) / math.log(MAX_DISTANCE / max_exact)
                         * (nb - max_exact)).astype(jnp.int32)
    large = jnp.minimum(large, nb - 1)
    return ret + jnp.where(n < max_exact, n, large)


def _toeplitz(w, n_rows, n_cols):
    period = w.shape[-1]
    flat = jnp.tile(w, (1,) * (w.ndim - 1) + (n_rows,))[..., :n_rows * (period - 1)]
    return flat.reshape(w.shape[:-1] + (n_rows, period - 1))[..., :n_cols]


def _bias_tables(rel_bias_table):
    tq, tk = ATTN_Q, ATTN_K
    assert tq == tk and tk % CHUNK == 0 and tk + 1 >= MAX_DISTANCE
    span = 2 * tk + N_META
    by_rel = rel_bias_table[_t5_bucket(jnp.arange(-span, span))].astype(F32).T
    far = by_rel[:, 0]

    period = 2 * tk
    signed = np.where(np.arange(period) < tk, np.arange(period), np.arange(period) - period)
    diag = jnp.stack([_toeplitz(by_rel[:, -signed - d * tk + span], tk, tq) for d in (0, 1)],
                     axis=1)
    key = np.arange(tk)[:, None]
    qry = np.arange(tq)[None, :]
    visible = np.stack([(key // CHUNK) <= (qry // CHUNK), np.ones((tk, tq), bool)])
    diag = jnp.where(visible[None], diag, NEG_INF)

    near = _toeplitz(by_rel[:, np.maximum(-signed - N_META, -span) + span], N_META, tq)
    meta = jnp.stack([near, jnp.broadcast_to(far[:, None, None], near.shape)], axis=1)
    return diag, meta, far


def kernel(x, meta_tokens, rel_bias_table, mix_norm_w, w_in, pool_group_w, pool_scale,
           lambda_q1, lambda_k1, lambda_q2, lambda_k2, subln_w, w_pool_out, w_attn_out,
           w_o, ffn_norm_w, w_gate, w_up, w_down, final_norm_w):
    batch, seq, d_model = x.shape
    assert w_in.shape[0] == 1, "single-layer block"
    pool_w = pool_scale.shape[1]
    attn_w = N_HEADS * V_DIM
    assert seq % ATTN_Q == 0 and seq % MERGE_ROWS == 0 and (batch * seq) % PROJ_ROWS == 0
    assert N_META >= max(POOL_WINDOWS) and MERGE_ROWS % N_META == 0

    w_in_bf = w_in[0].astype(BF16)
    x2d = x.reshape(batch * seq, d_model)
    u, q, k, v, gp, ga = _in_proj(x2d, mix_norm_w[0], w_in_bf, rows=PROJ_ROWS,
                                  pool_w=pool_w, attn_w=attn_w)
    u_meta, _, k_meta, v_meta, _, _ = _in_proj(meta_tokens.astype(x.dtype), mix_norm_w[0], w_in_bf,
                                               rows=N_META, pool_w=pool_w, attn_w=attn_w)
    v_meta_t = jnp.pad(v_meta.T, ((0, 0), (0, LANES - N_META)))

    bias_diag, bias_meta, far_bias = _bias_tables(rel_bias_table)
    lam_init = 0.8 - 0.6 * math.exp(-0.3 * LAYER_IDX)
    lam_vecs = jnp.stack([lambda_q1[0], lambda_k1[0], lambda_q2[0], lambda_k2[0]]).astype(F32)
    o = _diff_attn(q, k, v, k_meta, v_meta_t, bias_diag, bias_meta, far_bias, lam_vecs,
                   subln_w[0].reshape(V_DIM, 1).astype(F32),
                   batch=batch, seq=seq, lam_init=lam_init)

    d_ff = w_gate.shape[2]
    n_chunks = d_ff // FFN_COLS
    assert n_chunks * FFN_COLS == d_ff
    col_chunks = lambda w: w.astype(BF16).reshape(d_model, n_chunks, FFN_COLS).transpose(1, 0, 2)
    weights = (
        pool_group_w[0].astype(BF16), pool_scale[0].reshape(1, pool_w),
        w_pool_out[0].astype(BF16), w_attn_out[0].astype(BF16), w_o[0].astype(BF16),
        ffn_norm_w[0].reshape(1, d_model),
        col_chunks(w_gate[0]), col_chunks(w_up[0]),
        w_down[0].astype(BF16).reshape(n_chunks, FFN_COLS, d_model),
        final_norm_w.reshape(1, d_model))
    out = _merge_ffn(u, u_meta, x2d, o, gp, ga, weights, seq=seq)
    return out.reshape(batch, seq, d_model)
```

```python
import functools
import math

import jax
import jax.numpy as jnp
import numpy as np
from jax import lax
from jax.experimental import pallas as pl
from jax.experimental.pallas import tpu as pltpu

CHUNK = 64
N_META = 16
POOL_WINDOWS = (2, 4, 8, 16)
N_HEADS = 8
HEAD_DIM = 64
N_BUCKETS = 32
MAX_DISTANCE = 128
NORM_EPS = 1e-6
NEG_INF = -1e30
LAYER_IDX = 0

V_DIM = 2 * HEAD_DIM
LANES = 128

PROJ_ROWS = 512
ATTN_Q = 512
ATTN_K = 512
STAGES_PER_REGION = 4
MERGE_ROWS = 512
FFN_COLS = 256
VMEM_LIMIT = 56 * 1024 * 1024

BF16 = jnp.bfloat16
F32 = jnp.float32


def _const_spec(shape):
    return pl.BlockSpec(shape, lambda *_: (0,) * len(shape), pipeline_mode=pl.Buffered(1))


def _rms(x, w):
    var = jnp.mean(x * x, axis=-1, keepdims=True)
    return x * lax.rsqrt(var + NORM_EPS) * w


def _in_proj_kernel(x_ref, nw_ref, w_ref, u_ref, q_ref, k_ref, v_ref, gp_ref, ga_ref,
                    *, pool_w, attn_w, d_model):
    xn = _rms(x_ref[...], nw_ref[...]).astype(BF16)

    def proj(lo, width):
        return jnp.dot(xn, w_ref[:, lo:lo + width], preferred_element_type=F32)

    lo = 0
    u_ref[...] = proj(lo, pool_w)
    lo += pool_w
    q_ref[...] = (proj(lo, attn_w) * (HEAD_DIM ** -0.5)).astype(BF16)
    lo += attn_w
    k_ref[...] = proj(lo, attn_w).astype(BF16)
    lo += attn_w
    v_ref[...] = proj(lo, attn_w).astype(BF16)
    lo += attn_w
    gp_ref[...] = jax.nn.sigmoid(proj(lo, d_model))
    lo += d_model
    ga_ref[...] = jax.nn.sigmoid(proj(lo, d_model))


def _in_proj(x2d, norm_w, w_in_bf, *, rows, pool_w, attn_w):
    m, d_model = x2d.shape
    in_cols = w_in_bf.shape[1]
    row_spec = lambda width: pl.BlockSpec((rows, width), lambda i: (i, 0))
    out_shape = (
        jax.ShapeDtypeStruct((m, pool_w), F32),
        jax.ShapeDtypeStruct((m, attn_w), BF16),
        jax.ShapeDtypeStruct((m, attn_w), BF16),
        jax.ShapeDtypeStruct((m, attn_w), BF16),
        jax.ShapeDtypeStruct((m, d_model), F32),
        jax.ShapeDtypeStruct((m, d_model), F32),
    )
    return pl.pallas_call(
        functools.partial(_in_proj_kernel, pool_w=pool_w, attn_w=attn_w, d_model=d_model),
        grid=(m // rows,),
        in_specs=[row_spec(d_model), _const_spec((1, d_model)), _const_spec((d_model, in_cols))],
        out_specs=tuple(row_spec(s.shape[1]) for s in out_shape),
        out_shape=out_shape,
        compiler_params=pltpu.CompilerParams(
            dimension_semantics=("arbitrary",), vmem_limit_bytes=VMEM_LIMIT),
        name="in_proj",
    )(x2d, norm_w.reshape(1, d_model), w_in_bf)


def _attn_kernel(far_ref, q_ref, k_ref, v_ref, km_ref, vmt_ref, bd_ref, bm_ref, lam_ref, sw_ref,
                 o_ref, vt_sc, qs_sc, s_sc, mx_sc, m_sc, l_sc, acc_sc, *, lam_init):
    h = pl.program_id(1)
    n_blk, _, tk = vt_sc.shape
    tq = tk
    n_q = q_ref.shape[0] // tq
    far = far_ref[h]

    def transpose_values():
        for blk in range(n_blk):
            vt_sc[blk] = v_ref[blk * tk:(blk + 1) * tk, :].astype(F32).T.astype(BF16)

    def prologue(t):
        qt = q_ref[t * tq:(t + 1) * tq, :].astype(F32).T
        row = lax.broadcasted_iota(jnp.int32, qt.shape, 0)
        zero = jnp.zeros_like(qt)
        qs = jnp.concatenate([jnp.where(row < HEAD_DIM, qt, zero),
                              jnp.where(row >= HEAD_DIM, qt, zero)], axis=1).astype(BF16)
        qs_sc[t % 2] = qs
        bm = bm_ref[min(t, 1)]
        s = jnp.dot(km_ref[...], qs, preferred_element_type=F32)
        s = s + jnp.concatenate([bm, bm], axis=1)
        m0 = jnp.max(s, axis=0, keepdims=True)
        p = jnp.exp(s - m0)
        m_sc[t % 2] = m0
        l_sc[t % 2] = jnp.sum(p, axis=0, keepdims=True)
        p_pad = jnp.concatenate(
            [p.astype(BF16), jnp.zeros((vmt_ref.shape[1] - p.shape[0], p.shape[1]), BF16)], axis=0)
        acc_sc[t % 2] = jnp.dot(vmt_ref[...], p_pad, preferred_element_type=F32)

    def unit_a(slot, t, blk):
        s = jnp.dot(k_ref[blk * tk:(blk + 1) * tk, :], qs_sc[t % 2],
                    preferred_element_type=F32)
        if t - blk >= 2:
            shift = far
        else:
            bias = bd_ref[t - blk]
            s = s + jnp.concatenate([bias, bias], axis=1)
            shift = 0.0
        s_sc[slot] = s
        mx_sc[slot] = jnp.max(s, axis=0, keepdims=True) + shift

    def unit_b(slot, t, blk):
        shift = far if t - blk >= 2 else 0.0
        m_prev = m_sc[t % 2]
        m_new = jnp.maximum(m_prev, mx_sc[slot])
        alpha = jnp.exp(m_prev - m_new)
        p = jnp.exp(s_sc[slot] - (m_new - shift))
        l_sc[t % 2] = alpha * l_sc[t % 2] + jnp.sum(p, axis=0, keepdims=True)
        acc_sc[t % 2] = alpha * acc_sc[t % 2] + jnp.dot(vt_sc[blk], p.astype(BF16),
                                                        preferred_element_type=F32)
        m_sc[t % 2] = m_new

    def epilogue(t):
        lam = (jnp.exp(jnp.sum(lam_ref[0:1, :] * lam_ref[1:2, :], axis=1, keepdims=True))
               - jnp.exp(jnp.sum(lam_ref[2:3, :] * lam_ref[3:4, :], axis=1, keepdims=True))
               + lam_init)
        o = acc_sc[t % 2] / l_sc[t % 2]
        o = o[:, :tq] - lam * o[:, tq:]
        var = jnp.mean(o * o, axis=0, keepdims=True)
        o = o * lax.rsqrt(var + NORM_EPS) * sw_ref[...] * (1.0 - lam_init)
        o_ref[t * tq:(t + 1) * tq, :] = o.T.astype(o_ref.dtype)

    units = [(t, blk) for t in range(n_q) for blk in range(t + 1)]
    stages = [{"epilogue": [], "prologue": [], "a": [], "b": []} for _ in range(len(units) + 2)]
    for k, (t, blk) in enumerate(units):
        stages[k]["a"].append(functools.partial(unit_a, k % 2, t, blk))
        stages[k + 1]["b"].append(functools.partial(unit_b, k % 2, t, blk))
        if blk == 0:
            stages[max(k - 1, 0)]["prologue"].append(functools.partial(prologue, t))
        if blk == t:
            stages[k + 2]["epilogue"].append(functools.partial(epilogue, t))
    stages[0]["epilogue"].append(transpose_values)

    for first in range(0, len(stages), STAGES_PER_REGION):
        @pl.when(pl.program_id(0) >= -first)
        def _(group=stages[first:first + STAGES_PER_REGION]):
            for stage in group:
                for part in ("epilogue", "prologue", "a", "b"):
                    for fn in stage[part]:
                        fn()


def _diff_attn(q, k, v, k_meta, v_meta_t, bias_diag, bias_meta, far_bias, lam_vecs, subln_w,
               *, batch, seq, lam_init):
    n_meta = k_meta.shape[0]
    seq_spec = pl.BlockSpec((seq, V_DIM), lambda b, h: (b, h))
    return pl.pallas_call(
        functools.partial(_attn_kernel, lam_init=lam_init),
        grid=(batch, N_HEADS),
        in_specs=[
            pl.BlockSpec(memory_space=pltpu.SMEM),
            seq_spec, seq_spec, seq_spec,
            pl.BlockSpec((n_meta, V_DIM), lambda b, h: (0, h)),
            pl.BlockSpec((V_DIM, LANES), lambda b, h: (h, 0)),
            pl.BlockSpec((None, 2, ATTN_K, ATTN_Q), lambda b, h: (h, 0, 0, 0)),
            pl.BlockSpec((None, 2, n_meta, ATTN_Q), lambda b, h: (h, 0, 0, 0)),
            pl.BlockSpec((4, HEAD_DIM), lambda b, h: (0, 0)),
            pl.BlockSpec((V_DIM, 1), lambda b, h: (0, 0)),
        ],
        out_specs=seq_spec,
        out_shape=jax.ShapeDtypeStruct(q.shape, BF16),
        scratch_shapes=[
            pltpu.VMEM((seq // ATTN_K, V_DIM, ATTN_K), BF16),
            pltpu.VMEM((2, V_DIM, 2 * ATTN_Q), BF16),
            pltpu.VMEM((2, ATTN_K, 2 * ATTN_Q), F32),
            pltpu.VMEM((2, 1, 2 * ATTN_Q), F32),
            pltpu.VMEM((2, 1, 2 * ATTN_Q), F32),
            pltpu.VMEM((2, 1, 2 * ATTN_Q), F32),
            pltpu.VMEM((2, V_DIM, 2 * ATTN_Q), F32),
        ],
        compiler_params=pltpu.CompilerParams(
            dimension_semantics=("arbitrary", "arbitrary"),
            vmem_limit_bytes=VMEM_LIMIT),
        name="diff_attn",
    )(far_bias, q, k, v, k_meta, v_meta_t, bias_diag, bias_meta, lam_vecs, subln_w)


def _merge_ffn_kernel(u_ref, uprev_ref, umeta_ref, x_ref, o_ref, gp_ref, ga_ref,
                      gw_ref, ps_ref, wpo_ref, wao_ref, wo_ref, fnw_ref,
                      wg_ref, wu_ref, wd_ref, onw_ref, out_ref, ext_sc, hn_sc, acc_sc,
                      *, tiles_per_seq):
    rows = u_ref.shape[0]
    gdim = gw_ref.shape[1]
    hist = uprev_ref.shape[0]

    first = pl.program_id(0) % tiles_per_seq == 0
    ext_sc[0:hist, :] = jnp.where(first, umeta_ref[...], uprev_ref[...])
    ext_sc[hist:, :] = u_ref[...]

    pool_parts = []
    for g, win in enumerate(POOL_WINDOWS):
        cols = slice(g * gdim, (g + 1) * gdim)
        total = ext_sc[hist:hist + rows, cols]
        for back in range(1, win):
            total = total + ext_sc[hist - back:hist - back + rows, cols]
        pooled = total / float(win) - ext_sc[hist:hist + rows, cols]
        pool_parts.append(jnp.dot(pooled.astype(BF16), gw_ref[g], preferred_element_type=F32))
    pool_out = jnp.concatenate(pool_parts, axis=1) * ps_ref[...]

    merged = (gp_ref[...] * jnp.dot(pool_out.astype(BF16), wpo_ref[...],
                                    preferred_element_type=F32)
              + ga_ref[...] * jnp.dot(o_ref[...], wao_ref[...], preferred_element_type=F32))
    h1 = x_ref[...] + jnp.dot(merged.astype(BF16), wo_ref[...], preferred_element_type=F32)

    hn_sc[...] = _rms(h1, fnw_ref[...]).astype(BF16)
    acc_sc[...] = h1

    for c in range(0, wg_ref.shape[1], FFN_COLS):
        hn = hn_sc[...]
        gate = jnp.dot(hn, wg_ref[:, c:c + FFN_COLS], preferred_element_type=F32)
        up = jnp.dot(hn, wu_ref[:, c:c + FFN_COLS], preferred_element_type=F32)
        act = (jax.nn.silu(gate) * up).astype(BF16)
        acc_sc[...] += jnp.dot(act, wd_ref[c:c + FFN_COLS, :], preferred_element_type=F32)
    out_ref[...] = _rms(acc_sc[...], onw_ref[...])


def _merge_ffn(u, u_meta, x2d, o, gp, ga, weights, *, seq):
    m, d_model = x2d.shape
    pool_w = u.shape[1]
    hist = u_meta.shape[0]
    rows = MERGE_ROWS
    per_tile = rows // hist
    row_spec = lambda width: pl.BlockSpec((rows, width), lambda i: (i, 0))
    prev_spec = pl.BlockSpec((hist, pool_w), lambda i: (jnp.maximum(i * per_tile - 1, 0), 0))
    return pl.pallas_call(
        functools.partial(_merge_ffn_kernel, tiles_per_seq=seq // rows),
        grid=(m // rows,),
        in_specs=[row_spec(pool_w), prev_spec, _const_spec(u_meta.shape),
                  row_spec(d_model), row_spec(o.shape[1]), row_spec(d_model), row_spec(d_model)]
                 + [_const_spec(w.shape) for w in weights],
        out_specs=row_spec(d_model),
        out_shape=jax.ShapeDtypeStruct((m, d_model), F32),
        scratch_shapes=[pltpu.VMEM((hist + rows, pool_w), F32),
                        pltpu.VMEM((rows, d_model), BF16),
                        pltpu.VMEM((rows, d_model), F32)],
        compiler_params=pltpu.CompilerParams(
            dimension_semantics=("arbitrary",), vmem_limit_bytes=VMEM_LIMIT),
        name="merge_ffn",
    )(u, u, u_meta, x2d, o, gp, ga, *weights)


def _t5_bucket(rel):
    nb = N_BUCKETS // 2
    ret = jnp.where(rel > 0, nb, 0)
    n = jnp.abs(rel)
    max_exact = nb // 2
    nf = jnp.maximum(n, max_exact).astype(jnp.float32)
    large = max_exact + (jnp.log(nf / max_exact) / math.log(MAX_DISTANCE / max_exact)
                         * (nb - max_exact)).astype(jnp.int32)
    large = jnp.minimum(large, nb - 1)
    return ret + jnp.where(n < max_exact, n, large)


def _toeplitz(w, n_rows, n_cols):
    period = w.shape[-1]
    flat = jnp.tile(w, (1,) * (w.ndim - 1) + (n_rows,))[..., :n_rows * (period - 1)]
    return flat.reshape(w.shape[:-1] + (n_rows, period - 1))[..., :n_cols]


def _bias_tables(rel_bias_table):
    tq, tk = ATTN_Q, ATTN_K
    assert tq == tk and tk % CHUNK == 0 and tk + 1 >= MAX_DISTANCE
    span = 2 * tk + N_META
    by_rel = rel_bias_table[_t5_bucket(jnp.arange(-span, span))].astype(F32).T
    far = by_rel[:, 0]

    period = 2 * tk
    signed = np.where(np.arange(period) < tk, np.arange(period), np.arange(period) - period)
    diag = jnp.stack([_toeplitz(by_rel[:, -signed - d * tk + span], tk, tq) for d in (0, 1)],
                     axis=1)
    key = np.arange(tk)[:, None]
    qry = np.arange(tq)[None, :]
    visible = np.stack([(key // CHUNK) <= (qry // CHUNK), np.ones((tk, tq), bool)])
    diag = jnp.where(visible[None], diag, NEG_INF)

    near = _toeplitz(by_rel[:, np.maximum(-signed - N_META, -span) + span], N_META, tq)
    meta = jnp.stack([near, jnp.broadcast_to(far[:, None, None], near.shape)], axis=1)
    return diag, meta, far


def kernel(x, meta_tokens, rel_bias_table, mix_norm_w, w_in, pool_group_w, pool_scale,
           lambda_q1, lambda_k1, lambda_q2, lambda_k2, subln_w, w_pool_out, w_attn_out,
           w_o, ffn_norm_w, w_gate, w_up, w_down, final_norm_w):
    batch, seq, d_model = x.shape
    assert w_in.shape[0] == 1, "single-layer block"
    pool_w = pool_scale.shape[1]
    attn_w = N_HEADS * V_DIM
    assert seq % ATTN_Q == 0 and seq % MERGE_ROWS == 0 and (batch * seq) % PROJ_ROWS == 0
    assert N_META >= max(POOL_WINDOWS) and MERGE_ROWS % N_META == 0

    w_in_bf = w_in[0].astype(BF16)
    x2d = x.reshape(batch * seq, d_model)
    u, q, k, v, gp, ga = _in_proj(x2d, mix_norm_w[0], w_in_bf, rows=PROJ_ROWS,
                                  pool_w=pool_w, attn_w=attn_w)
    u_meta, _, k_meta, v_meta, _, _ = _in_proj(meta_tokens.astype(x.dtype), mix_norm_w[0], w_in_bf,
                                               rows=N_META, pool_w=pool_w, attn_w=attn_w)
    v_meta_t = jnp.pad(v_meta.T, ((0, 0), (0, LANES - N_META)))

    bias_diag, bias_meta, far_bias = _bias_tables(rel_bias_table)
    lam_init = 0.8 - 0.6 * math.exp(-0.3 * LAYER_IDX)
    lam_vecs = jnp.stack([lambda_q1[0], lambda_k1[0], lambda_q2[0], lambda_k2[0]]).astype(F32)
    o = _diff_attn(q, k, v, k_meta, v_meta_t, bias_diag, bias_meta, far_bias, lam_vecs,
                   subln_w[0].reshape(V_DIM, 1).astype(F32),
                   batch=batch, seq=seq, lam_init=lam_init)

    assert w_gate.shape[2] % FFN_COLS == 0
    weights = (
        pool_group_w[0].astype(BF16), pool_scale[0].reshape(1, pool_w),
        w_pool_out[0].astype(BF16), w_attn_out[0].astype(BF16), w_o[0].astype(BF16),
        ffn_norm_w[0].reshape(1, d_model),
        w_gate[0].astype(BF16), w_up[0].astype(BF16), w_down[0].astype(BF16),
        final_norm_w.reshape(1, d_model))
    out = _merge_ffn(u, u_meta, x2d, o, gp, ga, weights, seq=seq)
    return out.reshape(batch, seq, d_model)
```

```python
import functools
import math

import jax
import jax.numpy as jnp
import numpy as np
from jax import lax
from jax.experimental import pallas as pl
from jax.experimental.pallas import tpu as pltpu

CHUNK = 64
N_META = 16
POOL_WINDOWS = (2, 4, 8, 16)
N_HEADS = 8
HEAD_DIM = 64
N_BUCKETS = 32
MAX_DISTANCE = 128
NORM_EPS = 1e-6
NEG_INF = -1e30
LAYER_IDX = 0

V_DIM = 2 * HEAD_DIM
LANES = 128

PROJ_ROWS = 512
ATTN_Q = 512
ATTN_K = 512
STAGES_PER_REGION = 4
MERGE_ROWS = 512
FFN_COLS = 256
VMEM_LIMIT = 56 * 1024 * 1024

BF16 = jnp.bfloat16
F32 = jnp.float32


def _const_spec(shape):
    return pl.BlockSpec(shape, lambda *_: (0,) * len(shape), pipeline_mode=pl.Buffered(1))


def _rms(x, w):
    var = jnp.mean(x * x, axis=-1, keepdims=True)
    return x * lax.rsqrt(var + NORM_EPS) * w


def _in_proj_kernel(x_ref, nw_ref, w_ref, u_ref, q_ref, k_ref, v_ref, gp_ref, ga_ref,
                    *, pool_w, attn_w, d_model):
    xn = _rms(x_ref[...], nw_ref[...]).astype(BF16)

    def proj(lo, width):
        return jnp.dot(xn, w_ref[:, lo:lo + width], preferred_element_type=F32)

    lo = 0
    u_ref[...] = proj(lo, pool_w)
    lo += pool_w
    q_ref[...] = (proj(lo, attn_w) * (HEAD_DIM ** -0.5)).astype(BF16)
    lo += attn_w
    k_ref[...] = proj(lo, attn_w).astype(BF16)
    lo += attn_w
    v_ref[...] = proj(lo, attn_w).astype(BF16)
    lo += attn_w
    gp_ref[...] = jax.nn.sigmoid(proj(lo, d_model))
    lo += d_model
    ga_ref[...] = jax.nn.sigmoid(proj(lo, d_model))


def _in_proj(x2d, norm_w, w_in_bf, *, rows, pool_w, attn_w):
    m, d_model = x2d.shape
    in_cols = w_in_bf.shape[1]
    row_spec = lambda width: pl.BlockSpec((rows, width), lambda i: (i, 0))
    out_shape = (
        jax.ShapeDtypeStruct((m, pool_w), F32),
        jax.ShapeDtypeStruct((m, attn_w), BF16),
        jax.ShapeDtypeStruct((m, attn_w), BF16),
        jax.ShapeDtypeStruct((m, attn_w), BF16),
        jax.ShapeDtypeStruct((m, d_model), F32),
        jax.ShapeDtypeStruct((m, d_model), F32),
    )
    return pl.pallas_call(
        functools.partial(_in_proj_kernel, pool_w=pool_w, attn_w=attn_w, d_model=d_model),
        grid=(m // rows,),
        in_specs=[row_spec(d_model), _const_spec((1, d_model)), _const_spec((d_model, in_cols))],
        out_specs=tuple(row_spec(s.shape[1]) for s in out_shape),
        out_shape=out_shape,
        compiler_params=pltpu.CompilerParams(
            dimension_semantics=("arbitrary",), vmem_limit_bytes=VMEM_LIMIT),
        name="in_proj",
    )(x2d, norm_w.reshape(1, d_model), w_in_bf)


def _attn_kernel(far_ref, q_ref, k_ref, v_ref, km_ref, vmt_ref, bd_ref, bm_ref, lam_ref, sw_ref,
                 o_ref, vt_sc, qs_sc, s_sc, mx_sc, m_sc, l_sc, acc_sc, *, lam_init):
    h = pl.program_id(1)
    n_blk, _, tk = vt_sc.shape
    tq = tk
    n_q = q_ref.shape[0] // tq
    far = far_ref[h]

    def transpose_values():
        for blk in range(n_blk):
            vt_sc[blk] = v_ref[blk * tk:(blk + 1) * tk, :].astype(F32).T.astype(BF16)

    def prologue(t):
        qt = q_ref[t * tq:(t + 1) * tq, :].astype(F32).T
        row = lax.broadcasted_iota(jnp.int32, qt.shape, 0)
        zero = jnp.zeros_like(qt)
        qs = jnp.concatenate([jnp.where(row < HEAD_DIM, qt, zero),
                              jnp.where(row >= HEAD_DIM, qt, zero)], axis=1).astype(BF16)
        qs_sc[t % 2] = qs
        bm = bm_ref[min(t, 1)]
        s = jnp.dot(km_ref[...], qs, preferred_element_type=F32)
        s = s + jnp.concatenate([bm, bm], axis=1)
        m0 = jnp.max(s, axis=0, keepdims=True)
        p = jnp.exp(s - m0)
        m_sc[t % 2] = m0
        l_sc[t % 2] = jnp.sum(p, axis=0, keepdims=True)
        p_pad = jnp.concatenate(
            [p.astype(BF16), jnp.zeros((vmt_ref.shape[1] - p.shape[0], p.shape[1]), BF16)], axis=0)
        acc_sc[t % 2] = jnp.dot(vmt_ref[...], p_pad, preferred_element_type=F32)

    def unit_a(slot, t, blk):
        s = jnp.dot(k_ref[blk * tk:(blk + 1) * tk, :], qs_sc[t % 2],
                    preferred_element_type=F32)
        if t - blk >= 2:
            shift = far
        else:
            bias = bd_ref[t - blk]
            s = s + jnp.concatenate([bias, bias], axis=1)
            shift = 0.0
        s_sc[slot] = s
        mx_sc[slot] = jnp.max(s, axis=0, keepdims=True) + shift

    def unit_b(slot, t, blk):
        shift = far if t - blk >= 2 else 0.0
        m_prev = m_sc[t % 2]
        m_new = jnp.maximum(m_prev, mx_sc[slot])
        alpha = jnp.exp(m_prev - m_new)
        p = jnp.exp(s_sc[slot] - (m_new - shift))
        l_sc[t % 2] = alpha * l_sc[t % 2] + jnp.sum(p, axis=0, keepdims=True)
        acc_sc[t % 2] = alpha * acc_sc[t % 2] + jnp.dot(vt_sc[blk], p.astype(BF16),
                                                        preferred_element_type=F32)
        m_sc[t % 2] = m_new

    def epilogue(t):
        lam = (jnp.exp(jnp.sum(lam_ref[0:1, :] * lam_ref[1:2, :], axis=1, keepdims=True))
               - jnp.exp(jnp.sum(lam_ref[2:3, :] * lam_ref[3:4, :], axis=1, keepdims=True))
               + lam_init)
        o = acc_sc[t % 2] / l_sc[t % 2]
        o = o[:, :tq] - lam * o[:, tq:]
        var = jnp.mean(o * o, axis=0, keepdims=True)
        o = o * lax.rsqrt(var + NORM_EPS) * sw_ref[...] * (1.0 - lam_init)
        o_ref[t * tq:(t + 1) * tq, :] = o.T.astype(o_ref.dtype)

    units = [(t, blk) for t in range(n_q) for blk in range(t + 1)]
    stages = [{"epilogue": [], "prologue": [], "a": [], "b": []} for _ in range(len(units) + 2)]
    for k, (t, blk) in enumerate(units):
        stages[k]["a"].append(functools.partial(unit_a, k % 2, t, blk))
        stages[k + 1]["b"].append(functools.partial(unit_b, k % 2, t, blk))
        if blk == 0:
            stages[max(k - 1, 0)]["prologue"].append(functools.partial(prologue, t))
        if blk == t:
            stages[k + 2]["epilogue"].append(functools.partial(epilogue, t))
    stages[0]["epilogue"].append(transpose_values)

    for first in range(0, len(stages), STAGES_PER_REGION):
        @pl.when(pl.program_id(0) >= -first)
        def _(group=stages[first:first + STAGES_PER_REGION]):
            for stage in group:
                for part in ("epilogue", "prologue", "a", "b"):
                    for fn in stage[part]:
                        fn()


def _diff_attn(q, k, v, k_meta, v_meta_t, bias_diag, bias_meta, far_bias, lam_vecs, subln_w,
               *, batch, seq, lam_init):
    n_meta = k_meta.shape[0]
    seq_spec = pl.BlockSpec((seq, V_DIM), lambda b, h: (b, h))
    return pl.pallas_call(
        functools.partial(_attn_kernel, lam_init=lam_init),
        grid=(batch, N_HEADS),
        in_specs=[
            pl.BlockSpec(memory_space=pltpu.SMEM),
            seq_spec, seq_spec, seq_spec,
            pl.BlockSpec((n_meta, V_DIM), lambda b, h: (0, h)),
            pl.BlockSpec((V_DIM, LANES), lambda b, h: (h, 0)),
            pl.BlockSpec((None, 2, ATTN_K, ATTN_Q), lambda b, h: (h, 0, 0, 0)),
            pl.BlockSpec((None, 2, n_meta, ATTN_Q), lambda b, h: (h, 0, 0, 0)),
            pl.BlockSpec((4, HEAD_DIM), lambda b, h: (0, 0)),
            pl.BlockSpec((V_DIM, 1), lambda b, h: (0, 0)),
        ],
        out_specs=seq_spec,
        out_shape=jax.ShapeDtypeStruct(q.shape, BF16),
        scratch_shapes=[
            pltpu.VMEM((seq // ATTN_K, V_DIM, ATTN_K), BF16),
            pltpu.VMEM((2, V_DIM, 2 * ATTN_Q), BF16),
            pltpu.VMEM((2, ATTN_K, 2 * ATTN_Q), F32),
            pltpu.VMEM((2, 1, 2 * ATTN_Q), F32),
            pltpu.VMEM((2, 1, 2 * ATTN_Q), F32),
            pltpu.VMEM((2, 1, 2 * ATTN_Q), F32),
            pltpu.VMEM((2, V_DIM, 2 * ATTN_Q), F32),
        ],
        compiler_params=pltpu.CompilerParams(
            dimension_semantics=("arbitrary", "arbitrary"),
            vmem_limit_bytes=VMEM_LIMIT),
        name="diff_attn",
    )(far_bias, q, k, v, k_meta, v_meta_t, bias_diag, bias_meta, lam_vecs, subln_w)


def _merge_ffn_kernel(u_ref, uprev_ref, umeta_ref, x_ref, o_ref, gp_ref, ga_ref,
                      gw_ref, ps_ref, wpo_ref, wao_ref, wo_ref, fnw_ref,
                      wg_ref, wu_ref, wd_ref, onw_ref, out_ref, ext_sc, hn_sc, acc_sc,
                      *, tiles_per_seq):
    rows = u_ref.shape[0]
    gdim = gw_ref.shape[1]
    hist = uprev_ref.shape[0]

    first = pl.program_id(0) % tiles_per_seq == 0
    ext_sc[0:hist, :] = jnp.where(first, umeta_ref[...], uprev_ref[...])
    ext_sc[hist:, :] = u_ref[...]

    pool_parts = []
    for g, win in enumerate(POOL_WINDOWS):
        cols = slice(g * gdim, (g + 1) * gdim)
        total = ext_sc[hist:hist + rows, cols]
        for back in range(1, win):
            total = total + ext_sc[hist - back:hist - back + rows, cols]
        pooled = total / float(win) - ext_sc[hist:hist + rows, cols]
        pool_parts.append(jnp.dot(pooled.astype(BF16), gw_ref[g], preferred_element_type=F32))
    pool_out = jnp.concatenate(pool_parts, axis=1) * ps_ref[...]

    merged = (gp_ref[...] * jnp.dot(pool_out.astype(BF16), wpo_ref[...],
                                    preferred_element_type=F32)
              + ga_ref[...] * jnp.dot(o_ref[...], wao_ref[...], preferred_element_type=F32))
    h1 = x_ref[...] + jnp.dot(merged.astype(BF16), wo_ref[...], preferred_element_type=F32)

    hn_sc[...] = _rms(h1, fnw_ref[...]).astype(BF16)
    acc_sc[...] = h1

    for c in range(0, wg_ref.shape[1], FFN_COLS):
        hn = hn_sc[...]
        gate = jnp.dot(hn, wg_ref[:, c:c + FFN_COLS], preferred_element_type=F32)
        up = jnp.dot(hn, wu_ref[:, c:c + FFN_COLS], preferred_element_type=F32)
        act = (jax.nn.silu(gate) * up).astype(BF16)
        acc_sc[...] += jnp.dot(act, wd_ref[c:c + FFN_COLS, :], preferred_element_type=F32)
    out_ref[...] = _rms(acc_sc[...], onw_ref[...])


def _merge_ffn(u, u_meta, x2d, o, gp, ga, weights, *, seq):
    m, d_model = x2d.shape
    pool_w = u.shape[1]
    hist = u_meta.shape[0]
    rows = MERGE_ROWS
    per_tile = rows // hist
    row_spec = lambda width: pl.BlockSpec((rows, width), lambda i: (i, 0))
    prev_spec = pl.BlockSpec((hist, pool_w), lambda i: (jnp.maximum(i * per_tile - 1, 0), 0))
    return pl.pallas_call(
        functools.partial(_merge_ffn_kernel, tiles_per_seq=seq // rows),
        grid=(m // rows,),
        in_specs=[row_spec(pool_w), prev_spec, _const_spec(u_meta.shape),
                  row_spec(d_model), row_spec(o.shape[1]), row_spec(d_model), row_spec(d_model)]
                 + [_const_spec(w.shape) for w in weights],
        out_specs=row_spec(d_model),
        out_shape=jax.ShapeDtypeStruct((m, d_model), F32),
        scratch_shapes=[pltpu.VMEM((hist + rows, pool_w), F32),
                        pltpu.VMEM((rows, d_model), BF16),
                        pltpu.VMEM((rows, d_model), F32)],
        compiler_params=pltpu.CompilerParams(
            dimension_semantics=("arbitrary",), vmem_limit_bytes=VMEM_LIMIT),
        name="merge_ffn",
    )(u, u, u_meta, x2d, o, gp, ga, *weights)


def _t5_bucket(rel):
    nb = N_BUCKETS // 2
    ret = jnp.where(rel > 0, nb, 0)
    n = jnp.abs(rel)
    max_exact = nb // 2
    nf = jnp.maximum(n, max_exact).astype(jnp.float32)
    large = max_exact + (jnp.log(nf / max_exact) / math.log(MAX_DISTANCE / max_exact)
                         * (nb - max_exact)).astype(jnp.int32)
    large = jnp.minimum(large, nb - 1)
    return ret + jnp.where(n < max_exact, n, large)


def _bias_kernel(w_ref, diag_ref, meta_ref):
    _, tk, tq = diag_ref.shape
    n_meta = meta_ref.shape[1]
    period = w_ref.shape[1]

    def toeplitz(row, n_rows):
        x = jnp.broadcast_to(w_ref[row:row + 1, :], (n_rows, period))
        return pltpu.roll(x, 0, 1, stride=1, stride_axis=0)[:, :tq]

    key = lax.broadcasted_iota(jnp.int32, (tk, tq), 0)
    qry = lax.broadcasted_iota(jnp.int32, (tk, tq), 1)
    diag_ref[0] = jnp.where(key // CHUNK <= qry // CHUNK, toeplitz(0, tk), NEG_INF)
    diag_ref[1] = toeplitz(1, tk)
    meta_ref[0] = toeplitz(2, n_meta)
    meta_ref[1] = jnp.broadcast_to(w_ref[3:4, :tq], (n_meta, tq))


def _bias_tables(rel_bias_table):
    tq, tk = ATTN_Q, ATTN_K
    n_heads = rel_bias_table.shape[1]
    assert tq == tk and tk % CHUNK == 0 and tk + 1 >= MAX_DISTANCE
    period = 2 * tk
    signed = np.where(np.arange(period) < tk, np.arange(period), np.arange(period) - period)
    rel = np.stack([
        -signed,
        -signed - tk,
        -signed - N_META,
        np.full(period, -(tk + N_META)),
    ])
    vectors = rel_bias_table[_t5_bucket(jnp.asarray(rel))].astype(F32)
    vectors = jnp.transpose(vectors, (2, 0, 1))
    diag, meta = pl.pallas_call(
        _bias_kernel,
        grid=(n_heads,),
        in_specs=[pl.BlockSpec((None, 4, period), lambda h: (h, 0, 0))],
        out_specs=(pl.BlockSpec((None, 2, tk, tq), lambda h: (h, 0, 0, 0)),
                   pl.BlockSpec((None, 2, N_META, tq), lambda h: (h, 0, 0, 0))),
        out_shape=(jax.ShapeDtypeStruct((n_heads, 2, tk, tq), F32),
                   jax.ShapeDtypeStruct((n_heads, 2, N_META, tq), F32)),
        compiler_params=pltpu.CompilerParams(dimension_semantics=("arbitrary",)),
        name="bias_tiles",
    )(vectors)
    return diag, meta, vectors[:, 3, 0]


def kernel(x, meta_tokens, rel_bias_table, mix_norm_w, w_in, pool_group_w, pool_scale,
           lambda_q1, lambda_k1, lambda_q2, lambda_k2, subln_w, w_pool_out, w_attn_out,
           w_o, ffn_norm_w, w_gate, w_up, w_down, final_norm_w):
    batch, seq, d_model = x.shape
    assert w_in.shape[0] == 1, "single-layer block"
    pool_w = pool_scale.shape[1]
    attn_w = N_HEADS * V_DIM
    assert seq % ATTN_Q == 0 and seq % MERGE_ROWS == 0 and (batch * seq) % PROJ_ROWS == 0
    assert N_META >= max(POOL_WINDOWS) and MERGE_ROWS % N_META == 0

    w_in_bf = w_in[0].astype(BF16)
    x2d = x.reshape(batch * seq, d_model)
    u, q, k, v, gp, ga = _in_proj(x2d, mix_norm_w[0], w_in_bf, rows=PROJ_ROWS,
                                  pool_w=pool_w, attn_w=attn_w)
    u_meta, _, k_meta, v_meta, _, _ = _in_proj(meta_tokens.astype(x.dtype), mix_norm_w[0], w_in_bf,
                                               rows=N_META, pool_w=pool_w, attn_w=attn_w)
    v_meta_t = jnp.pad(v_meta.T, ((0, 0), (0, LANES - N_META)))

    bias_diag, bias_meta, far_bias = _bias_tables(rel_bias_table)
    lam_init = 0.8 - 0.6 * math.exp(-0.3 * LAYER_IDX)
    lam_vecs = jnp.stack([lambda_q1[0], lambda_k1[0], lambda_q2[0], lambda_k2[0]]).astype(F32)
    o = _diff_attn(q, k, v, k_meta, v_meta_t, bias_diag, bias_meta, far_bias, lam_vecs,
                   subln_w[0].reshape(V_DIM, 1).astype(F32),
                   batch=batch, seq=seq, lam_init=lam_init)

    assert w_gate.shape[2] % FFN_COLS == 0
    weights = (
        pool_group_w[0].astype(BF16), pool_scale[0].reshape(1, pool_w),
        w_pool_out[0].astype(BF16), w_attn_out[0].astype(BF16), w_o[0].astype(BF16),
        ffn_norm_w[0].reshape(1, d_model),
        w_gate[0].astype(BF16), w_up[0].astype(BF16), w_down[0].astype(BF16),
        final_norm_w.reshape(1, d_model))
    out = _merge_ffn(u, u_meta, x2d, o, gp, ga, weights, seq=seq)
    return out.reshape(batch, seq, d_model)
```

```python
import functools
import math

import jax
import jax.numpy as jnp
import numpy as np
from jax import lax
from jax.experimental import pallas as pl
from jax.experimental.pallas import tpu as pltpu

CHUNK = 64
N_META = 16
POOL_WINDOWS = (2, 4, 8, 16)
N_HEADS = 8
HEAD_DIM = 64
N_BUCKETS = 32
MAX_DISTANCE = 128
NORM_EPS = 1e-6
NEG_INF = -1e30
LAYER_IDX = 0

V_DIM = 2 * HEAD_DIM
LANES = 128
BF16_ROWS = 16
V_ROWS = V_DIM + BF16_ROWS

PROJ_ROWS = 512
ATTN_Q = 512
ATTN_K = 512
STAGES_PER_REGION = 4
COL_CHUNK = 256
MERGE_ROWS = 512
FFN_COLS = 256
VMEM_LIMIT = 56 * 1024 * 1024

BF16 = jnp.bfloat16
F32 = jnp.float32


def _const_spec(shape):
    return pl.BlockSpec(shape, lambda *_: (0,) * len(shape), pipeline_mode=pl.Buffered(1))


def _rms(x, w):
    var = jnp.mean(x * x, axis=-1, keepdims=True)
    return x * lax.rsqrt(var + NORM_EPS) * w


def _in_proj_kernel(x_ref, nw_ref, w_ref, u_ref, q_ref, k_ref, v_ref, gp_ref, ga_ref,
                    *, pool_w, attn_w, d_model):
    xn = _rms(x_ref[...], nw_ref[...]).astype(BF16)

    def proj(lo, width):
        return jnp.dot(xn, w_ref[:, lo:lo + width], preferred_element_type=F32)

    lo = 0
    u_ref[...] = proj(lo, pool_w)
    lo += pool_w
    q_ref[...] = (proj(lo, attn_w) * (HEAD_DIM ** -0.5)).astype(BF16)
    lo += attn_w
    k_ref[...] = proj(lo, attn_w).astype(BF16)
    lo += attn_w
    v_ref[...] = proj(lo, attn_w).astype(BF16)
    lo += attn_w
    gp_ref[...] = jax.nn.sigmoid(proj(lo, d_model))
    lo += d_model
    ga_ref[...] = jax.nn.sigmoid(proj(lo, d_model))


def _in_proj(x2d, norm_w, w_in_bf, *, rows, pool_w, attn_w):
    m, d_model = x2d.shape
    in_cols = w_in_bf.shape[1]
    row_spec = lambda width: pl.BlockSpec((rows, width), lambda i: (i, 0))
    out_shape = (
        jax.ShapeDtypeStruct((m, pool_w), F32),
        jax.ShapeDtypeStruct((m, attn_w), BF16),
        jax.ShapeDtypeStruct((m, attn_w), BF16),
        jax.ShapeDtypeStruct((m, attn_w), BF16),
        jax.ShapeDtypeStruct((m, d_model), F32),
        jax.ShapeDtypeStruct((m, d_model), F32),
    )
    return pl.pallas_call(
        functools.partial(_in_proj_kernel, pool_w=pool_w, attn_w=attn_w, d_model=d_model),
        grid=(m // rows,),
        in_specs=[row_spec(d_model), _const_spec((1, d_model)), _const_spec((d_model, in_cols))],
        out_specs=tuple(row_spec(s.shape[1]) for s in out_shape),
        out_shape=out_shape,
        compiler_params=pltpu.CompilerParams(
            dimension_semantics=("arbitrary",), vmem_limit_bytes=VMEM_LIMIT),
        name="in_proj",
    )(x2d, norm_w.reshape(1, d_model), w_in_bf)


def _attn_kernel(far_ref, q_ref, k_ref, v_ref, km_ref, vmt_ref, bd_ref, bm_ref, lam_ref, sw_ref,
                 o_ref, vt_sc, qs_sc, s_sc, mx_sc, m_sc, acc_sc, *, lam_init):
    h = pl.program_id(1)
    n_blk, v_rows, tk = vt_sc.shape
    tq = tk
    n_q = q_ref.shape[0] // tq
    far = far_ref[h]

    def transpose_values():
        ones_row = lax.broadcasted_iota(jnp.int32, (v_rows - V_DIM, tk), 0) == 0
        for blk in range(n_blk):
            vt_sc[blk, :V_DIM, :] = v_ref[blk * tk:(blk + 1) * tk, :].astype(F32).T.astype(BF16)
            vt_sc[blk, V_DIM:, :] = ones_row.astype(BF16)

    def prologue(t):
        qt = q_ref[t * tq:(t + 1) * tq, :].astype(F32).T
        row = lax.broadcasted_iota(jnp.int32, qt.shape, 0)
        zero = jnp.zeros_like(qt)
        qs = jnp.concatenate([jnp.where(row < HEAD_DIM, qt, zero),
                              jnp.where(row >= HEAD_DIM, qt, zero)], axis=1).astype(BF16)
        qs_sc[t % 2] = qs
        bm = bm_ref[min(t, 1)]
        s = jnp.dot(km_ref[...], qs, preferred_element_type=F32)
        s = s + jnp.concatenate([bm, bm], axis=1)
        m0 = jnp.max(s, axis=0, keepdims=True)
        p = jnp.exp(s - m0)
        m_sc[t % 2] = m0
        p_pad = jnp.concatenate(
            [p.astype(BF16), jnp.zeros((vmt_ref.shape[1] - p.shape[0], p.shape[1]), BF16)], axis=0)
        acc_sc[t % 2] = jnp.dot(vmt_ref[...], p_pad, preferred_element_type=F32)

    def unit_a(slot, t, blk):
        k_blk = k_ref[blk * tk:(blk + 1) * tk, :]
        for c in range(0, 2 * tq, COL_CHUNK):
            cols = slice(c, c + COL_CHUNK)
            s = jnp.dot(k_blk, qs_sc[t % 2, :, cols], preferred_element_type=F32)
            if t - blk >= 2:
                shift = far
            else:
                s = s + bd_ref[t - blk, :, c % tq:c % tq + COL_CHUNK]
                shift = 0.0
            s_sc[slot, :, cols] = s
            mx_sc[slot, :, cols] = jnp.max(s, axis=0, keepdims=True) + shift

    def unit_b(slot, t, blk):
        shift = far if t - blk >= 2 else 0.0
        for c in range(0, 2 * tq, COL_CHUNK):
            cols = slice(c, c + COL_CHUNK)
            m_prev = m_sc[t % 2, :, cols]
            m_new = jnp.maximum(m_prev, mx_sc[slot, :, cols])
            alpha = jnp.exp(m_prev - m_new)
            p = jnp.exp(s_sc[slot, :, cols] - (m_new - shift))
            acc_sc[t % 2, :, cols] = alpha * acc_sc[t % 2, :, cols] + jnp.dot(
                vt_sc[blk], p.astype(BF16), preferred_element_type=F32)
            m_sc[t % 2, :, cols] = m_new

    def epilogue(t):
        lam = (jnp.exp(jnp.sum(lam_ref[0:1, :] * lam_ref[1:2, :], axis=1, keepdims=True))
               - jnp.exp(jnp.sum(lam_ref[2:3, :] * lam_ref[3:4, :], axis=1, keepdims=True))
               + lam_init)
        o = acc_sc[t % 2, :V_DIM, :] / acc_sc[t % 2, V_DIM:V_DIM + 1, :]
        o = o[:, :tq] - lam * o[:, tq:]
        var = jnp.mean(o * o, axis=0, keepdims=True)
        o = o * lax.rsqrt(var + NORM_EPS) * sw_ref[...] * (1.0 - lam_init)
        o_ref[t * tq:(t + 1) * tq, :] = o.T.astype(o_ref.dtype)

    units = [(t, blk) for t in range(n_q) for blk in range(t + 1)]
    stages = [{"epilogue": [], "prologue": [], "a": [], "b": []} for _ in range(len(units) + 2)]
    for k, (t, blk) in enumerate(units):
        stages[k]["a"].append(functools.partial(unit_a, k % 2, t, blk))
        stages[k + 1]["b"].append(functools.partial(unit_b, k % 2, t, blk))
        if blk == 0:
            stages[max(k - 1, 0)]["prologue"].append(functools.partial(prologue, t))
        if blk == t:
            stages[k + 2]["epilogue"].append(functools.partial(epilogue, t))
    stages[0]["epilogue"].append(transpose_values)

    for first in range(0, len(stages), STAGES_PER_REGION):
        @pl.when(pl.program_id(0) >= -first)
        def _(group=stages[first:first + STAGES_PER_REGION]):
            for stage in group:
                for part in ("epilogue", "prologue", "a", "b"):
                    for fn in stage[part]:
                        fn()


def _diff_attn(q, k, v, k_meta, v_meta_t, bias_diag, bias_meta, far_bias, lam_vecs, subln_w,
               *, batch, seq, lam_init):
    n_meta = k_meta.shape[0]
    seq_spec = pl.BlockSpec((seq, V_DIM), lambda b, h: (b, h))
    return pl.pallas_call(
        functools.partial(_attn_kernel, lam_init=lam_init),
        grid=(batch, N_HEADS),
        in_specs=[
            pl.BlockSpec(memory_space=pltpu.SMEM),
            seq_spec, seq_spec, seq_spec,
            pl.BlockSpec((n_meta, V_DIM), lambda b, h: (0, h)),
            pl.BlockSpec((V_ROWS, LANES), lambda b, h: (h, 0)),
            pl.BlockSpec((None, 2, ATTN_K, ATTN_Q), lambda b, h: (h, 0, 0, 0)),
            pl.BlockSpec((None, 2, n_meta, ATTN_Q), lambda b, h: (h, 0, 0, 0)),
            pl.BlockSpec((4, HEAD_DIM), lambda b, h: (0, 0)),
            pl.BlockSpec((V_DIM, 1), lambda b, h: (0, 0)),
        ],
        out_specs=seq_spec,
        out_shape=jax.ShapeDtypeStruct(q.shape, BF16),
        scratch_shapes=[
            pltpu.VMEM((seq // ATTN_K, V_ROWS, ATTN_K), BF16),
            pltpu.VMEM((2, V_DIM, 2 * ATTN_Q), BF16),
            pltpu.VMEM((2, ATTN_K, 2 * ATTN_Q), F32),
            pltpu.VMEM((2, 1, 2 * ATTN_Q), F32),
            pltpu.VMEM((2, 1, 2 * ATTN_Q), F32),
            pltpu.VMEM((2, V_ROWS, 2 * ATTN_Q), F32),
        ],
        compiler_params=pltpu.CompilerParams(
            dimension_semantics=("arbitrary", "arbitrary"),
            vmem_limit_bytes=VMEM_LIMIT),
        name="diff_attn",
    )(far_bias, q, k, v, k_meta, v_meta_t, bias_diag, bias_meta, lam_vecs, subln_w)


def _merge_ffn_kernel(u_ref, uprev_ref, umeta_ref, x_ref, o_ref, gp_ref, ga_ref,
                      gw_ref, ps_ref, wpo_ref, wao_ref, wo_ref, fnw_ref,
                      wg_ref, wu_ref, wd_ref, onw_ref, out_ref, ext_sc, hn_sc, acc_sc,
                      *, tiles_per_seq):
    rows = u_ref.shape[0]
    gdim = gw_ref.shape[1]
    hist = uprev_ref.shape[0]

    first = pl.program_id(0) % tiles_per_seq == 0
    ext_sc[0:hist, :] = jnp.where(first, umeta_ref[...], uprev_ref[...])
    ext_sc[hist:, :] = u_ref[...]

    pool_parts = []
    for g, win in enumerate(POOL_WINDOWS):
        cols = slice(g * gdim, (g + 1) * gdim)
        total = ext_sc[hist:hist + rows, cols]
        for back in range(1, win):
            total = total + ext_sc[hist - back:hist - back + rows, cols]
        pooled = total / float(win) - ext_sc[hist:hist + rows, cols]
        pool_parts.append(jnp.dot(pooled.astype(BF16), gw_ref[g], preferred_element_type=F32))
    pool_out = jnp.concatenate(pool_parts, axis=1) * ps_ref[...]

    merged = (gp_ref[...] * jnp.dot(pool_out.astype(BF16), wpo_ref[...],
                                    preferred_element_type=F32)
              + ga_ref[...] * jnp.dot(o_ref[...], wao_ref[...], preferred_element_type=F32))
    h1 = x_ref[...] + jnp.dot(merged.astype(BF16), wo_ref[...], preferred_element_type=F32)

    hn_sc[...] = _rms(h1, fnw_ref[...]).astype(BF16)
    acc_sc[...] = h1

    for c in range(0, wg_ref.shape[1], FFN_COLS):
        hn = hn_sc[...]
        gate = jnp.dot(hn, wg_ref[:, c:c + FFN_COLS], preferred_element_type=F32)
        up = jnp.dot(hn, wu_ref[:, c:c + FFN_COLS], preferred_element_type=F32)
        act = (jax.nn.silu(gate) * up).astype(BF16)
        acc_sc[...] += jnp.dot(act, wd_ref[c:c + FFN_COLS, :], preferred_element_type=F32)
    out_ref[...] = _rms(acc_sc[...], onw_ref[...])


def _merge_ffn(u, u_meta, x2d, o, gp, ga, weights, *, seq):
    m, d_model = x2d.shape
    pool_w = u.shape[1]
    hist = u_meta.shape[0]
    rows = MERGE_ROWS
    per_tile = rows // hist
    row_spec = lambda width: pl.BlockSpec((rows, width), lambda i: (i, 0))
    prev_spec = pl.BlockSpec((hist, pool_w), lambda i: (jnp.maximum(i * per_tile - 1, 0), 0))
    return pl.pallas_call(
        functools.partial(_merge_ffn_kernel, tiles_per_seq=seq // rows),
        grid=(m // rows,),
        in_specs=[row_spec(pool_w), prev_spec, _const_spec(u_meta.shape),
                  row_spec(d_model), row_spec(o.shape[1]), row_spec(d_model), row_spec(d_model)]
                 + [_const_spec(w.shape) for w in weights],
        out_specs=row_spec(d_model),
        out_shape=jax.ShapeDtypeStruct((m, d_model), F32),
        scratch_shapes=[pltpu.VMEM((hist + rows, pool_w), F32),
                        pltpu.VMEM((rows, d_model), BF16),
                        pltpu.VMEM((rows, d_model), F32)],
        compiler_params=pltpu.CompilerParams(
            dimension_semantics=("arbitrary",), vmem_limit_bytes=VMEM_LIMIT),
        name="merge_ffn",
    )(u, u, u_meta, x2d, o, gp, ga, *weights)


def _t5_bucket(rel):
    nb = N_BUCKETS // 2
    ret = jnp.where(rel > 0, nb, 0)
    n = jnp.abs(rel)
    max_exact = nb // 2
    nf = jnp.maximum(n, max_exact).astype(jnp.float32)
    large = max_exact + (jnp.log(nf / max_exact) / math.log(MAX_DISTANCE / max_exact)
                         * (nb - max_exact)).astype(jnp.int32)
    large = jnp.minimum(large, nb - 1)
    return ret + jnp.where(n < max_exact, n, large)


def _bias_kernel(w_ref, diag_ref, meta_ref):
    _, tk, tq = diag_ref.shape
    n_meta = meta_ref.shape[1]
    period = w_ref.shape[1]

    def toeplitz(row, n_rows):
        x = jnp.broadcast_to(w_ref[row:row + 1, :], (n_rows, period))
        return pltpu.roll(x, 0, 1, stride=1, stride_axis=0)[:, :tq]

    key = lax.broadcasted_iota(jnp.int32, (tk, tq), 0)
    qry = lax.broadcasted_iota(jnp.int32, (tk, tq), 1)
    diag_ref[0] = jnp.where(key // CHUNK <= qry // CHUNK, toeplitz(0, tk), NEG_INF)
    diag_ref[1] = toeplitz(1, tk)
    meta_ref[0] = toeplitz(2, n_meta)
    meta_ref[1] = jnp.broadcast_to(w_ref[3:4, :tq], (n_meta, tq))


def _bias_tables(rel_bias_table):
    tq, tk = ATTN_Q, ATTN_K
    n_heads = rel_bias_table.shape[1]
    assert tq == tk and tk % CHUNK == 0 and tk + 1 >= MAX_DISTANCE
    period = 2 * tk
    signed = np.where(np.arange(period) < tk, np.arange(period), np.arange(period) - period)
    rel = np.stack([
        -signed,
        -signed - tk,
        -signed - N_META,
        np.full(period, -(tk + N_META)),
    ])
    vectors = rel_bias_table[_t5_bucket(jnp.asarray(rel))].astype(F32)
    vectors = jnp.transpose(vectors, (2, 0, 1))
    diag, meta = pl.pallas_call(
        _bias_kernel,
        grid=(n_heads,),
        in_specs=[pl.BlockSpec((None, 4, period), lambda h: (h, 0, 0))],
        out_specs=(pl.BlockSpec((None, 2, tk, tq), lambda h: (h, 0, 0, 0)),
                   pl.BlockSpec((None, 2, N_META, tq), lambda h: (h, 0, 0, 0))),
        out_shape=(jax.ShapeDtypeStruct((n_heads, 2, tk, tq), F32),
                   jax.ShapeDtypeStruct((n_heads, 2, N_META, tq), F32)),
        compiler_params=pltpu.CompilerParams(dimension_semantics=("arbitrary",)),
        name="bias_tiles",
    )(vectors)
    return diag, meta, vectors[:, 3, 0]


def kernel(x, meta_tokens, rel_bias_table, mix_norm_w, w_in, pool_group_w, pool_scale,
           lambda_q1, lambda_k1, lambda_q2, lambda_k2, subln_w, w_pool_out, w_attn_out,
           w_o, ffn_norm_w, w_gate, w_up, w_down, final_norm_w):
    batch, seq, d_model = x.shape
    assert w_in.shape[0] == 1, "single-layer block"
    pool_w = pool_scale.shape[1]
    attn_w = N_HEADS * V_DIM
    assert seq % ATTN_Q == 0 and seq % MERGE_ROWS == 0 and (batch * seq) % PROJ_ROWS == 0
    assert N_META >= max(POOL_WINDOWS) and MERGE_ROWS % N_META == 0

    w_in_bf = w_in[0].astype(BF16)
    x2d = x.reshape(batch * seq, d_model)
    u, q, k, v, gp, ga = _in_proj(x2d, mix_norm_w[0], w_in_bf, rows=PROJ_ROWS,
                                  pool_w=pool_w, attn_w=attn_w)
    u_meta, _, k_meta, v_meta, _, _ = _in_proj(meta_tokens.astype(x.dtype), mix_norm_w[0], w_in_bf,
                                               rows=N_META, pool_w=pool_w, attn_w=attn_w)
    v_meta_t = v_meta.T.reshape(N_HEADS, V_DIM, N_META)
    ones_rows = jnp.zeros((N_HEADS, BF16_ROWS, N_META), BF16).at[:, 0, :].set(1)
    v_meta_t = jnp.pad(jnp.concatenate([v_meta_t, ones_rows], axis=1),
                       ((0, 0), (0, 0), (0, LANES - N_META))).reshape(N_HEADS * V_ROWS, LANES)

    bias_diag, bias_meta, far_bias = _bias_tables(rel_bias_table)
    lam_init = 0.8 - 0.6 * math.exp(-0.3 * LAYER_IDX)
    lam_vecs = jnp.stack([lambda_q1[0], lambda_k1[0], lambda_q2[0], lambda_k2[0]]).astype(F32)
    o = _diff_attn(q, k, v, k_meta, v_meta_t, bias_diag, bias_meta, far_bias, lam_vecs,
                   subln_w[0].reshape(V_DIM, 1).astype(F32),
                   batch=batch, seq=seq, lam_init=lam_init)

    assert w_gate.shape[2] % FFN_COLS == 0
    weights = (
        pool_group_w[0].astype(BF16), pool_scale[0].reshape(1, pool_w),
        w_pool_out[0].astype(BF16), w_attn_out[0].astype(BF16), w_o[0].astype(BF16),
        ffn_norm_w[0].reshape(1, d_model),
        w_gate[0].astype(BF16), w_up[0].astype(BF16), w_down[0].astype(BF16),
        final_norm_w.reshape(1, d_model))
    out = _merge_ffn(u, u_meta, x2d, o, gp, ga, weights, seq=seq)
    return out.reshape(batch, seq, d_model)
```

```python
import functools
import math

import jax
import jax.numpy as jnp
import numpy as np
from jax import lax
from jax.experimental import pallas as pl
from jax.experimental.pallas import tpu as pltpu

CHUNK = 64
N_META = 16
POOL_WINDOWS = (2, 4, 8, 16)
N_HEADS = 8
HEAD_DIM = 64
N_BUCKETS = 32
MAX_DISTANCE = 128
NORM_EPS = 1e-6
NEG_INF = -1e30
LOG2_E = math.log2(math.e)
LAYER_IDX = 0

V_DIM = 2 * HEAD_DIM
LANES = 128
BF16_ROWS = 16
V_ROWS = V_DIM + BF16_ROWS

PROJ_ROWS = 512
ATTN_Q = 512
ATTN_K = 512
STAGES_PER_REGION = 6
COL_CHUNK = 256
MERGE_ROWS = 512
FFN_COLS = 256
VMEM_LIMIT = 56 * 1024 * 1024

BF16 = jnp.bfloat16
F32 = jnp.float32


def _const_spec(shape):
    return pl.BlockSpec(shape, lambda *_: (0,) * len(shape), pipeline_mode=pl.Buffered(1))


def _rms(x, w):
    var = jnp.mean(x * x, axis=-1, keepdims=True)
    return x * lax.rsqrt(var + NORM_EPS) * w


def _in_proj_kernel(x_ref, nw_ref, w_ref, u_ref, q_ref, k_ref, v_ref, gp_ref, ga_ref,
                    *, pool_w, attn_w, d_model):
    xn = _rms(x_ref[...], nw_ref[...]).astype(BF16)

    def proj(lo, width):
        return jnp.dot(xn, w_ref[:, lo:lo + width], preferred_element_type=F32)

    lo = 0
    u_ref[...] = proj(lo, pool_w)
    lo += pool_w
    q_ref[...] = (proj(lo, attn_w) * (HEAD_DIM ** -0.5)).astype(BF16)
    lo += attn_w
    k_ref[...] = proj(lo, attn_w).astype(BF16)
    lo += attn_w
    v_ref[...] = proj(lo, attn_w).astype(BF16)
    lo += attn_w
    gp_ref[...] = jax.nn.sigmoid(proj(lo, d_model))
    lo += d_model
    ga_ref[...] = jax.nn.sigmoid(proj(lo, d_model))


def _in_proj(x2d, norm_w, w_in_bf, *, rows, pool_w, attn_w):
    m, d_model = x2d.shape
    in_cols = w_in_bf.shape[1]
    row_spec = lambda width: pl.BlockSpec((rows, width), lambda i: (i, 0))
    out_shape = (
        jax.ShapeDtypeStruct((m, pool_w), F32),
        jax.ShapeDtypeStruct((m, attn_w), BF16),
        jax.ShapeDtypeStruct((m, attn_w), BF16),
        jax.ShapeDtypeStruct((m, attn_w), BF16),
        jax.ShapeDtypeStruct((m, d_model), F32),
        jax.ShapeDtypeStruct((m, d_model), F32),
    )
    return pl.pallas_call(
        functools.partial(_in_proj_kernel, pool_w=pool_w, attn_w=attn_w, d_model=d_model),
        grid=(m // rows,),
        in_specs=[row_spec(d_model), _const_spec((1, d_model)), _const_spec((d_model, in_cols))],
        out_specs=tuple(row_spec(s.shape[1]) for s in out_shape),
        out_shape=out_shape,
        compiler_params=pltpu.CompilerParams(
            dimension_semantics=("arbitrary",), vmem_limit_bytes=VMEM_LIMIT),
        name="in_proj",
    )(x2d, norm_w.reshape(1, d_model), w_in_bf)


def _attn_kernel(far_ref, q_ref, k_ref, v_ref, km_ref, vmt_ref, bd_ref, bm_ref, lam_ref, sw_ref,
                 o_ref, vt_sc, qs_sc, s_sc, mx_sc, m_sc, acc_sc, *, lam_init):
    h = pl.program_id(1)
    n_blk, v_rows, tk = vt_sc.shape
    tq = tk
    n_q = q_ref.shape[0] // tq
    far = far_ref[h]

    def transpose_values():
        ones_row = lax.broadcasted_iota(jnp.int32, (v_rows - V_DIM, tk), 0) == 0
        for blk in range(n_blk):
            vt_sc[blk, :V_DIM, :] = v_ref[blk * tk:(blk + 1) * tk, :].astype(F32).T.astype(BF16)
            vt_sc[blk, V_DIM:, :] = ones_row.astype(BF16)

    def prologue(t):
        qt = q_ref[t * tq:(t + 1) * tq, :].astype(F32).T * LOG2_E
        row = lax.broadcasted_iota(jnp.int32, qt.shape, 0)
        zero = jnp.zeros_like(qt)
        qt = jnp.concatenate([jnp.where(row < HEAD_DIM, qt, zero),
                              jnp.where(row >= HEAD_DIM, qt, zero)], axis=1)
        hi = qt.astype(BF16)
        lo = (qt - hi.astype(F32)).astype(BF16)
        qs = jnp.concatenate([hi, lo], axis=0)
        qs_sc[t % 2] = qs
        bm = bm_ref[min(t, 1)]
        km = km_ref[...]
        s = jnp.dot(jnp.concatenate([km, km], axis=1), qs, preferred_element_type=F32)
        s = s + jnp.concatenate([bm, bm], axis=1)
        m0 = jnp.max(s, axis=0, keepdims=True)
        p = jnp.exp2(s - m0)
        m_sc[t % 2] = m0
        p_pad = jnp.concatenate(
            [p.astype(BF16), jnp.zeros((vmt_ref.shape[1] - p.shape[0], p.shape[1]), BF16)], axis=0)
        acc_sc[t % 2] = jnp.dot(vmt_ref[...], p_pad, preferred_element_type=F32)

    def unit_a(slot, t, blk):
        k_blk = k_ref[blk * tk:(blk + 1) * tk, :]
        k_blk = jnp.concatenate([k_blk, k_blk], axis=1)
        for c in range(0, 2 * tq, COL_CHUNK):
            cols = slice(c, c + COL_CHUNK)
            s = jnp.dot(k_blk, qs_sc[t % 2, :, cols], preferred_element_type=F32)
            if t - blk >= 2:
                shift = far
            else:
                s = s + bd_ref[t - blk, :, c % tq:c % tq + COL_CHUNK]
                shift = 0.0
            s_sc[slot, :, cols] = s
            mx_sc[slot, :, cols] = jnp.max(s, axis=0, keepdims=True) + shift

    def unit_b(slot, t, blk):
        shift = far if t - blk >= 2 else 0.0
        for c in range(0, 2 * tq, COL_CHUNK):
            cols = slice(c, c + COL_CHUNK)
            m_prev = m_sc[t % 2, :, cols]
            m_new = jnp.maximum(m_prev, mx_sc[slot, :, cols])
            alpha = jnp.exp2(m_prev - m_new)
            p = jnp.exp2(s_sc[slot, :, cols] - (m_new - shift))
            acc_sc[t % 2, :, cols] = alpha * acc_sc[t % 2, :, cols] + jnp.dot(
                vt_sc[blk], p.astype(BF16), preferred_element_type=F32)
            m_sc[t % 2, :, cols] = m_new

    def epilogue(t):
        lam = (jnp.exp(jnp.sum(lam_ref[0:1, :] * lam_ref[1:2, :], axis=1, keepdims=True))
               - jnp.exp(jnp.sum(lam_ref[2:3, :] * lam_ref[3:4, :], axis=1, keepdims=True))
               + lam_init)
        o = acc_sc[t % 2, :V_DIM, :] / acc_sc[t % 2, V_DIM:V_DIM + 1, :]
        o = o[:, :tq] - lam * o[:, tq:]
        var = jnp.mean(o * o, axis=0, keepdims=True)
        o = o * lax.rsqrt(var + NORM_EPS) * sw_ref[...] * (1.0 - lam_init)
        o_ref[t * tq:(t + 1) * tq, :] = o.T.astype(o_ref.dtype)

    units = [(t, blk) for t in range(n_q) for blk in range(t + 1)]
    stages = [{"epilogue": [], "prologue": [], "a": [], "b": []} for _ in range(len(units) + 2)]
    for k, (t, blk) in enumerate(units):
        stages[k]["a"].append(functools.partial(unit_a, k % 2, t, blk))
        stages[k + 1]["b"].append(functools.partial(unit_b, k % 2, t, blk))
        if blk == 0:
            stages[max(k - 1, 0)]["prologue"].append(functools.partial(prologue, t))
        if blk == t:
            stages[k + 2]["epilogue"].append(functools.partial(epilogue, t))
    stages[0]["epilogue"].append(transpose_values)

    for first in range(0, len(stages), STAGES_PER_REGION):
        @pl.when(pl.program_id(0) >= -first)
        def _(group=stages[first:first + STAGES_PER_REGION]):
            for stage in group:
                for part in ("epilogue", "prologue", "a", "b"):
                    for fn in stage[part]:
                        fn()


def _diff_attn(q, k, v, k_meta, v_meta_t, bias_diag, bias_meta, far_bias, lam_vecs, subln_w,
               *, batch, seq, lam_init):
    n_meta = k_meta.shape[0]
    seq_spec = pl.BlockSpec((seq, V_DIM), lambda b, h: (b, h))
    return pl.pallas_call(
        functools.partial(_attn_kernel, lam_init=lam_init),
        grid=(batch, N_HEADS),
        in_specs=[
            pl.BlockSpec(memory_space=pltpu.SMEM),
            seq_spec, seq_spec, seq_spec,
            pl.BlockSpec((n_meta, V_DIM), lambda b, h: (0, h)),
            pl.BlockSpec((V_ROWS, LANES), lambda b, h: (h, 0)),
            pl.BlockSpec((None, 2, ATTN_K, ATTN_Q), lambda b, h: (h, 0, 0, 0)),
            pl.BlockSpec((None, 2, n_meta, ATTN_Q), lambda b, h: (h, 0, 0, 0)),
            pl.BlockSpec((4, HEAD_DIM), lambda b, h: (0, 0)),
            pl.BlockSpec((V_DIM, 1), lambda b, h: (0, 0)),
        ],
        out_specs=seq_spec,
        out_shape=jax.ShapeDtypeStruct(q.shape, BF16),
        scratch_shapes=[
            pltpu.VMEM((seq // ATTN_K, V_ROWS, ATTN_K), BF16),
            pltpu.VMEM((2, 2 * V_DIM, 2 * ATTN_Q), BF16),
            pltpu.VMEM((2, ATTN_K, 2 * ATTN_Q), F32),
            pltpu.VMEM((2, 1, 2 * ATTN_Q), F32),
            pltpu.VMEM((2, 1, 2 * ATTN_Q), F32),
            pltpu.VMEM((2, V_ROWS, 2 * ATTN_Q), F32),
        ],
        compiler_params=pltpu.CompilerParams(
            dimension_semantics=("arbitrary", "arbitrary"),
            vmem_limit_bytes=VMEM_LIMIT),
        name="diff_attn",
    )(far_bias, q, k, v, k_meta, v_meta_t, bias_diag, bias_meta, lam_vecs, subln_w)


def _merge_ffn_kernel(u_ref, uprev_ref, umeta_ref, x_ref, o_ref, gp_ref, ga_ref,
                      gw_ref, ps_ref, wpo_ref, wao_ref, wo_ref, fnw_ref,
                      wg_ref, wu_ref, wd_ref, onw_ref, out_ref, ext_sc, hn_sc, acc_sc,
                      *, tiles_per_seq):
    rows = u_ref.shape[0]
    gdim = gw_ref.shape[1]
    hist = uprev_ref.shape[0]

    first = pl.program_id(0) % tiles_per_seq == 0
    ext_sc[0:hist, :] = jnp.where(first, umeta_ref[...], uprev_ref[...])
    ext_sc[hist:, :] = u_ref[...]

    pool_parts = []
    for g, win in enumerate(POOL_WINDOWS):
        cols = slice(g * gdim, (g + 1) * gdim)
        total = ext_sc[hist:hist + rows, cols]
        for back in range(1, win):
            total = total + ext_sc[hist - back:hist - back + rows, cols]
        pooled = total / float(win) - ext_sc[hist:hist + rows, cols]
        pool_parts.append(jnp.dot(pooled.astype(BF16), gw_ref[g], preferred_element_type=F32))
    pool_out = jnp.concatenate(pool_parts, axis=1) * ps_ref[...]

    merged = (gp_ref[...] * jnp.dot(pool_out.astype(BF16), wpo_ref[...],
                                    preferred_element_type=F32)
              + ga_ref[...] * jnp.dot(o_ref[...], wao_ref[...], preferred_element_type=F32))
    h1 = x_ref[...] + jnp.dot(merged.astype(BF16), wo_ref[...], preferred_element_type=F32)

    hn_sc[...] = _rms(h1, fnw_ref[...]).astype(BF16)
    acc_sc[...] = h1

    for c in range(0, wg_ref.shape[1], FFN_COLS):
        hn = hn_sc[...]
        gate = jnp.dot(hn, wg_ref[:, c:c + FFN_COLS], preferred_element_type=F32)
        up = jnp.dot(hn, wu_ref[:, c:c + FFN_COLS], preferred_element_type=F32)
        act = (jax.nn.silu(gate) * up).astype(BF16)
        acc_sc[...] += jnp.dot(act, wd_ref[c:c + FFN_COLS, :], preferred_element_type=F32)
    out_ref[...] = _rms(acc_sc[...], onw_ref[...])


def _merge_ffn(u, u_meta, x2d, o, gp, ga, weights, *, seq):
    m, d_model = x2d.shape
    pool_w = u.shape[1]
    hist = u_meta.shape[0]
    rows = MERGE_ROWS
    per_tile = rows // hist
    row_spec = lambda width: pl.BlockSpec((rows, width), lambda i: (i, 0))
    prev_spec = pl.BlockSpec((hist, pool_w), lambda i: (jnp.maximum(i * per_tile - 1, 0), 0))
    return pl.pallas_call(
        functools.partial(_merge_ffn_kernel, tiles_per_seq=seq // rows),
        grid=(m // rows,),
        in_specs=[row_spec(pool_w), prev_spec, _const_spec(u_meta.shape),
                  row_spec(d_model), row_spec(o.shape[1]), row_spec(d_model), row_spec(d_model)]
                 + [_const_spec(w.shape) for w in weights],
        out_specs=row_spec(d_model),
        out_shape=jax.ShapeDtypeStruct((m, d_model), F32),
        scratch_shapes=[pltpu.VMEM((hist + rows, pool_w), F32),
                        pltpu.VMEM((rows, d_model), BF16),
                        pltpu.VMEM((rows, d_model), F32)],
        compiler_params=pltpu.CompilerParams(
            dimension_semantics=("arbitrary",), vmem_limit_bytes=VMEM_LIMIT),
        name="merge_ffn",
    )(u, u, u_meta, x2d, o, gp, ga, *weights)


def _t5_bucket(rel):
    nb = N_BUCKETS // 2
    ret = jnp.where(rel > 0, nb, 0)
    n = jnp.abs(rel)
    max_exact = nb // 2
    nf = jnp.maximum(n, max_exact).astype(jnp.float32)
    large = max_exact + (jnp.log(nf / max_exact) / math.log(MAX_DISTANCE / max_exact)
                         * (nb - max_exact)).astype(jnp.int32)
    large = jnp.minimum(large, nb - 1)
    return ret + jnp.where(n < max_exact, n, large)


def _bias_kernel(w_ref, diag_ref, meta_ref):
    _, tk, tq = diag_ref.shape
    n_meta = meta_ref.shape[1]
    period = w_ref.shape[1]

    def toeplitz(row, n_rows):
        x = jnp.broadcast_to(w_ref[row:row + 1, :], (n_rows, period))
        return pltpu.roll(x, 0, 1, stride=1, stride_axis=0)[:, :tq]

    key = lax.broadcasted_iota(jnp.int32, (tk, tq), 0)
    qry = lax.broadcasted_iota(jnp.int32, (tk, tq), 1)
    diag_ref[0] = jnp.where(key // CHUNK <= qry // CHUNK, toeplitz(0, tk), NEG_INF)
    diag_ref[1] = toeplitz(1, tk)
    meta_ref[0] = toeplitz(2, n_meta)
    meta_ref[1] = jnp.broadcast_to(w_ref[3:4, :tq], (n_meta, tq))


def _bias_tables(rel_bias_table):
    tq, tk = ATTN_Q, ATTN_K
    n_heads = rel_bias_table.shape[1]
    assert tq == tk and tk % CHUNK == 0 and tk + 1 >= MAX_DISTANCE
    period = 2 * tk
    signed = np.where(np.arange(period) < tk, np.arange(period), np.arange(period) - period)
    rel = np.stack([
        -signed,
        -signed - tk,
        -signed - N_META,
        np.full(period, -(tk + N_META)),
    ])
    vectors = rel_bias_table[_t5_bucket(jnp.asarray(rel))].astype(F32) * LOG2_E
    vectors = jnp.transpose(vectors, (2, 0, 1))
    diag, meta = pl.pallas_call(
        _bias_kernel,
        grid=(n_heads,),
        in_specs=[pl.BlockSpec((None, 4, period), lambda h: (h, 0, 0))],
        out_specs=(pl.BlockSpec((None, 2, tk, tq), lambda h: (h, 0, 0, 0)),
                   pl.BlockSpec((None, 2, N_META, tq), lambda h: (h, 0, 0, 0))),
        out_shape=(jax.ShapeDtypeStruct((n_heads, 2, tk, tq), F32),
                   jax.ShapeDtypeStruct((n_heads, 2, N_META, tq), F32)),
        compiler_params=pltpu.CompilerParams(dimension_semantics=("arbitrary",)),
        name="bias_tiles",
    )(vectors)
    return diag, meta, vectors[:, 3, 0]


def kernel(x, meta_tokens, rel_bias_table, mix_norm_w, w_in, pool_group_w, pool_scale,
           lambda_q1, lambda_k1, lambda_q2, lambda_k2, subln_w, w_pool_out, w_attn_out,
           w_o, ffn_norm_w, w_gate, w_up, w_down, final_norm_w):
    batch, seq, d_model = x.shape
    assert w_in.shape[0] == 1, "single-layer block"
    pool_w = pool_scale.shape[1]
    attn_w = N_HEADS * V_DIM
    assert seq % ATTN_Q == 0 and seq % MERGE_ROWS == 0 and (batch * seq) % PROJ_ROWS == 0
    assert N_META >= max(POOL_WINDOWS) and MERGE_ROWS % N_META == 0

    w_in_bf = w_in[0].astype(BF16)
    x2d = x.reshape(batch * seq, d_model)
    u, q, k, v, gp, ga = _in_proj(x2d, mix_norm_w[0], w_in_bf, rows=PROJ_ROWS,
                                  pool_w=pool_w, attn_w=attn_w)
    u_meta, _, k_meta, v_meta, _, _ = _in_proj(meta_tokens.astype(x.dtype), mix_norm_w[0], w_in_bf,
                                               rows=N_META, pool_w=pool_w, attn_w=attn_w)
    v_meta_t = v_meta.T.reshape(N_HEADS, V_DIM, N_META)
    ones_rows = jnp.zeros((N_HEADS, BF16_ROWS, N_META), BF16).at[:, 0, :].set(1)
    v_meta_t = jnp.pad(jnp.concatenate([v_meta_t, ones_rows], axis=1),
                       ((0, 0), (0, 0), (0, LANES - N_META))).reshape(N_HEADS * V_ROWS, LANES)

    bias_diag, bias_meta, far_bias = _bias_tables(rel_bias_table)
    lam_init = 0.8 - 0.6 * math.exp(-0.3 * LAYER_IDX)
    lam_vecs = jnp.stack([lambda_q1[0], lambda_k1[0], lambda_q2[0], lambda_k2[0]]).astype(F32)
    o = _diff_attn(q, k, v, k_meta, v_meta_t, bias_diag, bias_meta, far_bias, lam_vecs,
                   subln_w[0].reshape(V_DIM, 1).astype(F32),
                   batch=batch, seq=seq, lam_init=lam_init)

    assert w_gate.shape[2] % FFN_COLS == 0
    weights = (
        pool_group_w[0].astype(BF16), pool_scale[0].reshape(1, pool_w),
        w_pool_out[0].astype(BF16), w_attn_out[0].astype(BF16), w_o[0].astype(BF16),
        ffn_norm_w[0].reshape(1, d_model),
        w_gate[0].astype(BF16), w_up[0].astype(BF16), w_down[0].astype(BF16),
        final_norm_w.reshape(1, d_model))
    out = _merge_ffn(u, u_meta, x2d, o, gp, ga, weights, seq=seq)
    return out.reshape(batch, seq, d_model)
```

```python
import functools
import math

import jax
import jax.numpy as jnp
import numpy as np
from jax import lax
from jax.experimental import pallas as pl
from jax.experimental.pallas import tpu as pltpu

CHUNK = 64
N_META = 16
POOL_WINDOWS = (2, 4, 8, 16)
N_HEADS = 8
HEAD_DIM = 64
N_BUCKETS = 32
MAX_DISTANCE = 128
NORM_EPS = 1e-6
NEG_INF = -1e30
LOG2_E = math.log2(math.e)
LAYER_IDX = 0

V_DIM = 2 * HEAD_DIM
LANES = 128
BF16_ROWS = 16
V_ROWS = V_DIM + BF16_ROWS

PROJ_ROWS = 512
ATTN_Q = 512
ATTN_K = 512
STAGES_PER_REGION = 6
COL_CHUNK = 256
MERGE_ROWS = 512
FFN_COLS = 256
VMEM_LIMIT = 56 * 1024 * 1024

BF16 = jnp.bfloat16
F32 = jnp.float32


def _const_spec(shape):
    return pl.BlockSpec(shape, lambda *_: (0,) * len(shape), pipeline_mode=pl.Buffered(1))


def _rms(x, w):
    var = jnp.mean(x * x, axis=-1, keepdims=True)
    return x * lax.rsqrt(var + NORM_EPS) * w


def _in_proj_kernel(x_ref, nw_ref, w_ref, u_ref, q_ref, k_ref, v_ref, gp_ref, ga_ref,
                    *, pool_w, attn_w, d_model):
    xn = _rms(x_ref[...], nw_ref[...]).astype(BF16)

    def proj(lo, width):
        return jnp.dot(xn, w_ref[:, lo:lo + width], preferred_element_type=F32)

    lo = 0
    u_ref[...] = proj(lo, pool_w)
    lo += pool_w
    q_ref[...] = (proj(lo, attn_w) * (HEAD_DIM ** -0.5)).astype(BF16)
    lo += attn_w
    k_ref[...] = proj(lo, attn_w).astype(BF16)
    lo += attn_w
    v_ref[...] = proj(lo, attn_w).astype(BF16)
    lo += attn_w
    gp_ref[...] = jax.nn.sigmoid(proj(lo, d_model))
    lo += d_model
    ga_ref[...] = jax.nn.sigmoid(proj(lo, d_model))


def _in_proj(x2d, norm_w, w_in_bf, *, rows, pool_w, attn_w):
    m, d_model = x2d.shape
    in_cols = w_in_bf.shape[1]
    row_spec = lambda width: pl.BlockSpec((rows, width), lambda i: (i, 0))
    out_shape = (
        jax.ShapeDtypeStruct((m, pool_w), F32),
        jax.ShapeDtypeStruct((m, attn_w), BF16),
        jax.ShapeDtypeStruct((m, attn_w), BF16),
        jax.ShapeDtypeStruct((m, attn_w), BF16),
        jax.ShapeDtypeStruct((m, d_model), F32),
        jax.ShapeDtypeStruct((m, d_model), F32),
    )
    return pl.pallas_call(
        functools.partial(_in_proj_kernel, pool_w=pool_w, attn_w=attn_w, d_model=d_model),
        grid=(m // rows,),
        in_specs=[row_spec(d_model), _const_spec((1, d_model)), _const_spec((d_model, in_cols))],
        out_specs=tuple(row_spec(s.shape[1]) for s in out_shape),
        out_shape=out_shape,
        compiler_params=pltpu.CompilerParams(
            dimension_semantics=("arbitrary",), vmem_limit_bytes=VMEM_LIMIT),
        name="in_proj",
    )(x2d, norm_w.reshape(1, d_model), w_in_bf)


def _attn_kernel(far_ref, q_ref, k_ref, v_ref, km_ref, vmt_ref, bd_ref, bm_ref, lam_ref, sw_ref,
                 o_ref, vt_sc, qs_sc, s_sc, mx_sc, m_sc, acc_sc, *, lam_init):
    h = pl.program_id(1)
    n_blk, v_rows, tk = vt_sc.shape
    tq = tk
    n_q = q_ref.shape[0] // tq
    far = far_ref[h]

    def transpose_values():
        ones_row = lax.broadcasted_iota(jnp.int32, (v_rows - V_DIM, tk), 0) == 0
        for blk in range(n_blk):
            vt_sc[blk, :V_DIM, :] = v_ref[blk * tk:(blk + 1) * tk, :].astype(F32).T.astype(BF16)
            vt_sc[blk, V_DIM:, :] = ones_row.astype(BF16)

    def prologue(t):
        qt = q_ref[t * tq:(t + 1) * tq, :].astype(F32).T * LOG2_E
        row = lax.broadcasted_iota(jnp.int32, qt.shape, 0)
        zero = jnp.zeros_like(qt)
        qt = jnp.concatenate([jnp.where(row < HEAD_DIM, qt, zero),
                              jnp.where(row >= HEAD_DIM, qt, zero)], axis=1)
        hi = qt.astype(BF16)
        lo = (qt - hi.astype(F32)).astype(BF16)
        qs = jnp.concatenate([hi, lo], axis=0)
        qs_sc[t % 2] = qs
        bm = bm_ref[min(t, 1)]
        km = km_ref[...]
        s = jnp.dot(jnp.concatenate([km, km], axis=1), qs, preferred_element_type=F32)
        s = s + jnp.concatenate([bm, bm], axis=1)
        m0 = jnp.max(s, axis=0, keepdims=True)
        p = jnp.exp2(s - m0)
        m_sc[t % 2] = m0
        p_pad = jnp.concatenate(
            [p.astype(BF16), jnp.zeros((vmt_ref.shape[1] - p.shape[0], p.shape[1]), BF16)], axis=0)
        acc_sc[t % 2] = jnp.dot(vmt_ref[...], p_pad, preferred_element_type=F32)

    def live_keys(t, blk, c):
        return min(tk, c % tq + COL_CHUNK) if t == blk else tk

    def unit_a(slot, t, blk):
        k_blk = k_ref[blk * tk:(blk + 1) * tk, :]
        k_blk = jnp.concatenate([k_blk, k_blk], axis=1)
        for c in range(0, 2 * tq, COL_CHUNK):
            cols = slice(c, c + COL_CHUNK)
            n_keys = live_keys(t, blk, c)
            s = jnp.dot(k_blk[:n_keys], qs_sc[t % 2, :, cols], preferred_element_type=F32)
            if t - blk >= 2:
                shift = far
            else:
                s = s + bd_ref[t - blk, :n_keys, c % tq:c % tq + COL_CHUNK]
                shift = 0.0
            s_sc[slot, :n_keys, cols] = s
            mx_sc[slot, :, cols] = jnp.max(s, axis=0, keepdims=True) + shift

    def unit_b(slot, t, blk):
        shift = far if t - blk >= 2 else 0.0
        for c in range(0, 2 * tq, COL_CHUNK):
            cols = slice(c, c + COL_CHUNK)
            n_keys = live_keys(t, blk, c)
            m_prev = m_sc[t % 2, :, cols]
            m_new = jnp.maximum(m_prev, mx_sc[slot, :, cols])
            alpha = jnp.exp2(m_prev - m_new)
            p = jnp.exp2(s_sc[slot, :n_keys, cols] - (m_new - shift))
            acc_sc[t % 2, :, cols] = alpha * acc_sc[t % 2, :, cols] + jnp.dot(
                vt_sc[blk, :, :n_keys], p.astype(BF16), preferred_element_type=F32)
            m_sc[t % 2, :, cols] = m_new

    def epilogue(t):
        lam = (jnp.exp(jnp.sum(lam_ref[0:1, :] * lam_ref[1:2, :], axis=1, keepdims=True))
               - jnp.exp(jnp.sum(lam_ref[2:3, :] * lam_ref[3:4, :], axis=1, keepdims=True))
               + lam_init)
        o = acc_sc[t % 2, :V_DIM, :] / acc_sc[t % 2, V_DIM:V_DIM + 1, :]
        o = o[:, :tq] - lam * o[:, tq:]
        var = jnp.mean(o * o, axis=0, keepdims=True)
        o = o * lax.rsqrt(var + NORM_EPS) * sw_ref[...] * (1.0 - lam_init)
        o_ref[t * tq:(t + 1) * tq, :] = o.T.astype(o_ref.dtype)

    units = [(t, blk) for t in range(n_q) for blk in range(t + 1)]
    stages = [{"epilogue": [], "prologue": [], "a": [], "b": []} for _ in range(len(units) + 2)]
    for k, (t, blk) in enumerate(units):
        stages[k]["a"].append(functools.partial(unit_a, k % 2, t, blk))
        stages[k + 1]["b"].append(functools.partial(unit_b, k % 2, t, blk))
        if blk == 0:
            stages[max(k - 1, 0)]["prologue"].append(functools.partial(prologue, t))
        if blk == t:
            stages[k + 2]["epilogue"].append(functools.partial(epilogue, t))
    stages[0]["epilogue"].append(transpose_values)

    for first in range(0, len(stages), STAGES_PER_REGION):
        @pl.when(pl.program_id(0) >= -first)
        def _(group=stages[first:first + STAGES_PER_REGION]):
            for stage in group:
                for part in ("epilogue", "prologue", "a", "b"):
                    for fn in stage[part]:
                        fn()


def _diff_attn(q, k, v, k_meta, v_meta_t, bias_diag, bias_meta, far_bias, lam_vecs, subln_w,
               *, batch, seq, lam_init):
    n_meta = k_meta.shape[0]
    seq_spec = pl.BlockSpec((seq, V_DIM), lambda b, h: (b, h))
    return pl.pallas_call(
        functools.partial(_attn_kernel, lam_init=lam_init),
        grid=(batch, N_HEADS),
        in_specs=[
            pl.BlockSpec(memory_space=pltpu.SMEM),
            seq_spec, seq_spec, seq_spec,
            pl.BlockSpec((n_meta, V_DIM), lambda b, h: (0, h)),
            pl.BlockSpec((V_ROWS, LANES), lambda b, h: (h, 0)),
            pl.BlockSpec((None, 2, ATTN_K, ATTN_Q), lambda b, h: (h, 0, 0, 0)),
            pl.BlockSpec((None, 2, n_meta, ATTN_Q), lambda b, h: (h, 0, 0, 0)),
            pl.BlockSpec((4, HEAD_DIM), lambda b, h: (0, 0)),
            pl.BlockSpec((V_DIM, 1), lambda b, h: (0, 0)),
        ],
        out_specs=seq_spec,
        out_shape=jax.ShapeDtypeStruct(q.shape, BF16),
        scratch_shapes=[
            pltpu.VMEM((seq // ATTN_K, V_ROWS, ATTN_K), BF16),
            pltpu.VMEM((2, 2 * V_DIM, 2 * ATTN_Q), BF16),
            pltpu.VMEM((2, ATTN_K, 2 * ATTN_Q), F32),
            pltpu.VMEM((2, 1, 2 * ATTN_Q), F32),
            pltpu.VMEM((2, 1, 2 * ATTN_Q), F32),
            pltpu.VMEM((2, V_ROWS, 2 * ATTN_Q), F32),
        ],
        compiler_params=pltpu.CompilerParams(
            dimension_semantics=("arbitrary", "arbitrary"),
            vmem_limit_bytes=VMEM_LIMIT),
        name="diff_attn",
    )(far_bias, q, k, v, k_meta, v_meta_t, bias_diag, bias_meta, lam_vecs, subln_w)


def _merge_ffn_kernel(u_ref, uprev_ref, umeta_ref, x_ref, o_ref, gp_ref, ga_ref,
                      gw_ref, ps_ref, wpo_ref, wao_ref, wo_ref, fnw_ref,
                      wg_ref, wu_ref, wd_ref, onw_ref, out_ref, ext_sc, hn_sc, acc_sc,
                      *, tiles_per_seq):
    rows = u_ref.shape[0]
    gdim = gw_ref.shape[1]
    hist = uprev_ref.shape[0]

    first = pl.program_id(0) % tiles_per_seq == 0
    ext_sc[0:hist, :] = jnp.where(first, umeta_ref[...], uprev_ref[...])
    ext_sc[hist:, :] = u_ref[...]

    pool_parts = []
    for g, win in enumerate(POOL_WINDOWS):
        cols = slice(g * gdim, (g + 1) * gdim)
        total = ext_sc[hist:hist + rows, cols]
        for back in range(1, win):
            total = total + ext_sc[hist - back:hist - back + rows, cols]
        pooled = total / float(win) - ext_sc[hist:hist + rows, cols]
        pool_parts.append(jnp.dot(pooled.astype(BF16), gw_ref[g], preferred_element_type=F32))
    pool_out = jnp.concatenate(pool_parts, axis=1) * ps_ref[...]

    merged = (gp_ref[...] * jnp.dot(pool_out.astype(BF16), wpo_ref[...],
                                    preferred_element_type=F32)
              + ga_ref[...] * jnp.dot(o_ref[...], wao_ref[...], preferred_element_type=F32))
    h1 = x_ref[...] + jnp.dot(merged.astype(BF16), wo_ref[...], preferred_element_type=F32)

    hn_sc[...] = _rms(h1, fnw_ref[...]).astype(BF16)
    acc_sc[...] = h1

    for c in range(0, wg_ref.shape[1], FFN_COLS):
        hn = hn_sc[...]
        gate = jnp.dot(hn, wg_ref[:, c:c + FFN_COLS], preferred_element_type=F32)
        up = jnp.dot(hn, wu_ref[:, c:c + FFN_COLS], preferred_element_type=F32)
        act = (jax.nn.silu(gate) * up).astype(BF16)
        acc_sc[...] += jnp.dot(act, wd_ref[c:c + FFN_COLS, :], preferred_element_type=F32)
    out_ref[...] = _rms(acc_sc[...], onw_ref[...])


def _merge_ffn(u, u_meta, x2d, o, gp, ga, weights, *, seq):
    m, d_model = x2d.shape
    pool_w = u.shape[1]
    hist = u_meta.shape[0]
    rows = MERGE_ROWS
    per_tile = rows // hist
    row_spec = lambda width: pl.BlockSpec((rows, width), lambda i: (i, 0))
    prev_spec = pl.BlockSpec((hist, pool_w), lambda i: (jnp.maximum(i * per_tile - 1, 0), 0))
    return pl.pallas_call(
        functools.partial(_merge_ffn_kernel, tiles_per_seq=seq // rows),
        grid=(m // rows,),
        in_specs=[row_spec(pool_w), prev_spec, _const_spec(u_meta.shape),
                  row_spec(d_model), row_spec(o.shape[1]), row_spec(d_model), row_spec(d_model)]
                 + [_const_spec(w.shape) for w in weights],
        out_specs=row_spec(d_model),
        out_shape=jax.ShapeDtypeStruct((m, d_model), F32),
        scratch_shapes=[pltpu.VMEM((hist + rows, pool_w), F32),
                        pltpu.VMEM((rows, d_model), BF16),
                        pltpu.VMEM((rows, d_model), F32)],
        compiler_params=pltpu.CompilerParams(
            dimension_semantics=("arbitrary",), vmem_limit_bytes=VMEM_LIMIT),
        name="merge_ffn",
    )(u, u, u_meta, x2d, o, gp, ga, *weights)


def _t5_bucket(rel):
    nb = N_BUCKETS // 2
    ret = jnp.where(rel > 0, nb, 0)
    n = jnp.abs(rel)
    max_exact = nb // 2
    nf = jnp.maximum(n, max_exact).astype(jnp.float32)
    large = max_exact + (jnp.log(nf / max_exact) / math.log(MAX_DISTANCE / max_exact)
                         * (nb - max_exact)).astype(jnp.int32)
    large = jnp.minimum(large, nb - 1)
    return ret + jnp.where(n < max_exact, n, large)


def _bias_kernel(w_ref, diag_ref, meta_ref):
    _, tk, tq = diag_ref.shape
    n_meta = meta_ref.shape[1]
    period = w_ref.shape[1]

    def toeplitz(row, n_rows):
        x = jnp.broadcast_to(w_ref[row:row + 1, :], (n_rows, period))
        return pltpu.roll(x, 0, 1, stride=1, stride_axis=0)[:, :tq]

    key = lax.broadcasted_iota(jnp.int32, (tk, tq), 0)
    qry = lax.broadcasted_iota(jnp.int32, (tk, tq), 1)
    diag_ref[0] = jnp.where(key // CHUNK <= qry // CHUNK, toeplitz(0, tk), NEG_INF)
    diag_ref[1] = toeplitz(1, tk)
    meta_ref[0] = toeplitz(2, n_meta)
    meta_ref[1] = jnp.broadcast_to(w_ref[3:4, :tq], (n_meta, tq))


def _bias_tables(rel_bias_table):
    tq, tk = ATTN_Q, ATTN_K
    n_heads = rel_bias_table.shape[1]
    assert tq == tk and tk % CHUNK == 0 and tk + 1 >= MAX_DISTANCE
    period = 2 * tk
    signed = np.where(np.arange(period) < tk, np.arange(period), np.arange(period) - period)
    rel = np.stack([
        -signed,
        -signed - tk,
        -signed - N_META,
        np.full(period, -(tk + N_META)),
    ])
    vectors = rel_bias_table[_t5_bucket(jnp.asarray(rel))].astype(F32) * LOG2_E
    vectors = jnp.transpose(vectors, (2, 0, 1))
    diag, meta = pl.pallas_call(
        _bias_kernel,
        grid=(n_heads,),
        in_specs=[pl.BlockSpec((None, 4, period), lambda h: (h, 0, 0))],
        out_specs=(pl.BlockSpec((None, 2, tk, tq), lambda h: (h, 0, 0, 0)),
                   pl.BlockSpec((None, 2, N_META, tq), lambda h: (h, 0, 0, 0))),
        out_shape=(jax.ShapeDtypeStruct((n_heads, 2, tk, tq), F32),
                   jax.ShapeDtypeStruct((n_heads, 2, N_META, tq), F32)),
        compiler_params=pltpu.CompilerParams(dimension_semantics=("arbitrary",)),
        name="bias_tiles",
    )(vectors)
    return diag, meta, vectors[:, 3, 0]


def kernel(x, meta_tokens, rel_bias_table, mix_norm_w, w_in, pool_group_w, pool_scale,
           lambda_q1, lambda_k1, lambda_q2, lambda_k2, subln_w, w_pool_out, w_attn_out,
           w_o, ffn_norm_w, w_gate, w_up, w_down, final_norm_w):
    batch, seq, d_model = x.shape
    assert w_in.shape[0] == 1, "single-layer block"
    pool_w = pool_scale.shape[1]
    attn_w = N_HEADS * V_DIM
    assert seq % ATTN_Q == 0 and seq % MERGE_ROWS == 0 and (batch * seq) % PROJ_ROWS == 0
    assert N_META >= max(POOL_WINDOWS) and MERGE_ROWS % N_META == 0

    w_in_bf = w_in[0].astype(BF16)
    x2d = x.reshape(batch * seq, d_model)
    u, q, k, v, gp, ga = _in_proj(x2d, mix_norm_w[0], w_in_bf, rows=PROJ_ROWS,
                                  pool_w=pool_w, attn_w=attn_w)
    u_meta, _, k_meta, v_meta, _, _ = _in_proj(meta_tokens.astype(x.dtype), mix_norm_w[0], w_in_bf,
                                               rows=N_META, pool_w=pool_w, attn_w=attn_w)
    v_meta_t = v_meta.T.reshape(N_HEADS, V_DIM, N_META)
    ones_rows = jnp.zeros((N_HEADS, BF16_ROWS, N_META), BF16).at[:, 0, :].set(1)
    v_meta_t = jnp.pad(jnp.concatenate([v_meta_t, ones_rows], axis=1),
                       ((0, 0), (0, 0), (0, LANES - N_META))).reshape(N_HEADS * V_ROWS, LANES)

    bias_diag, bias_meta, far_bias = _bias_tables(rel_bias_table)
    lam_init = 0.8 - 0.6 * math.exp(-0.3 * LAYER_IDX)
    lam_vecs = jnp.stack([lambda_q1[0], lambda_k1[0], lambda_q2[0], lambda_k2[0]]).astype(F32)
    o = _diff_attn(q, k, v, k_meta, v_meta_t, bias_diag, bias_meta, far_bias, lam_vecs,
                   subln_w[0].reshape(V_DIM, 1).astype(F32),
                   batch=batch, seq=seq, lam_init=lam_init)

    assert w_gate.shape[2] % FFN_COLS == 0
    weights = (
        pool_group_w[0].astype(BF16), pool_scale[0].reshape(1, pool_w),
        w_pool_out[0].astype(BF16), w_attn_out[0].astype(BF16), w_o[0].astype(BF16),
        ffn_norm_w[0].reshape(1, d_model),
        w_gate[0].astype(BF16), w_up[0].astype(BF16), w_down[0].astype(BF16),
        final_norm_w.reshape(1, d_model))
    out = _merge_ffn(u, u_meta, x2d, o, gp, ga, weights, seq=seq)
    return out.reshape(batch, seq, d_model)
```

```python
import functools
import math

import jax
import jax.numpy as jnp
import numpy as np
from jax import lax
from jax.experimental import pallas as pl
from jax.experimental.pallas import tpu as pltpu

CHUNK = 64
N_META = 16
POOL_WINDOWS = (2, 4, 8, 16)
N_HEADS = 8
HEAD_DIM = 64
N_BUCKETS = 32
MAX_DISTANCE = 128
NORM_EPS = 1e-6
NEG_INF = -1e30
LOG2_E = math.log2(math.e)
LAYER_IDX = 0

V_DIM = 2 * HEAD_DIM
LANES = 128
BF16_ROWS = 16
V_ROWS = V_DIM + BF16_ROWS

PROJ_ROWS = 512
ATTN_Q = 512
ATTN_K = 512
STAGES_PER_REGION = 13
COL_CHUNK = 256
MERGE_ROWS = 512
FFN_COLS = 256
VMEM_LIMIT = 56 * 1024 * 1024

BF16 = jnp.bfloat16
F32 = jnp.float32


def _const_spec(shape):
    return pl.BlockSpec(shape, lambda *_: (0,) * len(shape), pipeline_mode=pl.Buffered(1))


def _rms(x, w):
    var = jnp.mean(x * x, axis=-1, keepdims=True)
    return x * lax.rsqrt(var + NORM_EPS) * w


def _in_proj_kernel(x_ref, nw_ref, w_ref, u_ref, q_ref, k_ref, v_ref, gp_ref, ga_ref,
                    *, pool_w, attn_w, d_model):
    xn = _rms(x_ref[...], nw_ref[...]).astype(BF16)

    def proj(lo, width):
        return jnp.dot(xn, w_ref[:, lo:lo + width], preferred_element_type=F32)

    lo = 0
    u_ref[...] = proj(lo, pool_w)
    lo += pool_w
    q_ref[...] = (proj(lo, attn_w) * (HEAD_DIM ** -0.5)).astype(BF16)
    lo += attn_w
    k_ref[...] = proj(lo, attn_w).astype(BF16)
    lo += attn_w
    v_ref[...] = proj(lo, attn_w).astype(BF16)
    lo += attn_w
    gp_ref[...] = jax.nn.sigmoid(proj(lo, d_model))
    lo += d_model
    ga_ref[...] = jax.nn.sigmoid(proj(lo, d_model))


def _in_proj(x2d, norm_w, w_in_bf, *, rows, pool_w, attn_w):
    m, d_model = x2d.shape
    in_cols = w_in_bf.shape[1]
    row_spec = lambda width: pl.BlockSpec((rows, width), lambda i: (i, 0))
    out_shape = (
        jax.ShapeDtypeStruct((m, pool_w), F32),
        jax.ShapeDtypeStruct((m, attn_w), BF16),
        jax.ShapeDtypeStruct((m, attn_w), BF16),
        jax.ShapeDtypeStruct((m, attn_w), BF16),
        jax.ShapeDtypeStruct((m, d_model), F32),
        jax.ShapeDtypeStruct((m, d_model), F32),
    )
    return pl.pallas_call(
        functools.partial(_in_proj_kernel, pool_w=pool_w, attn_w=attn_w, d_model=d_model),
        grid=(m // rows,),
        in_specs=[row_spec(d_model), _const_spec((1, d_model)), _const_spec((d_model, in_cols))],
        out_specs=tuple(row_spec(s.shape[1]) for s in out_shape),
        out_shape=out_shape,
        compiler_params=pltpu.CompilerParams(
            dimension_semantics=("arbitrary",), vmem_limit_bytes=VMEM_LIMIT),
        name="in_proj",
    )(x2d, norm_w.reshape(1, d_model), w_in_bf)


def _attn_kernel(far_ref, q_ref, k_ref, v_ref, km_ref, vmt_ref, bd_ref, bm_ref, lam_ref, sw_ref,
                 o_ref, vt_sc, qs_sc, s_sc, mx_sc, m_sc, acc_sc, *, lam_init):
    h = pl.program_id(1)
    n_blk, v_rows, tk = vt_sc.shape
    tq = tk
    n_q = q_ref.shape[0] // tq
    far = far_ref[h]

    def transpose_values():
        ones_row = lax.broadcasted_iota(jnp.int32, (v_rows - V_DIM, tk), 0) == 0
        for blk in range(n_blk):
            vt_sc[blk, :V_DIM, :] = v_ref[blk * tk:(blk + 1) * tk, :].astype(F32).T.astype(BF16)
            vt_sc[blk, V_DIM:, :] = ones_row.astype(BF16)

    def prologue(t):
        qt = q_ref[t * tq:(t + 1) * tq, :].astype(F32).T * LOG2_E
        row = lax.broadcasted_iota(jnp.int32, qt.shape, 0)
        zero = jnp.zeros_like(qt)
        qt = jnp.concatenate([jnp.where(row < HEAD_DIM, qt, zero),
                              jnp.where(row >= HEAD_DIM, qt, zero)], axis=1)
        hi = qt.astype(BF16)
        lo = (qt - hi.astype(F32)).astype(BF16)
        qs = jnp.concatenate([hi, lo], axis=0)
        qs_sc[t % 2] = qs
        bm = bm_ref[min(t, 1)]
        km = km_ref[...]
        s = jnp.dot(jnp.concatenate([km, km], axis=1), qs, preferred_element_type=F32)
        s = s + jnp.concatenate([bm, bm], axis=1)
        m0 = jnp.max(s, axis=0, keepdims=True)
        p = jnp.exp2(s - m0)
        m_sc[t % 2] = m0
        p_pad = jnp.concatenate(
            [p.astype(BF16), jnp.zeros((vmt_ref.shape[1] - p.shape[0], p.shape[1]), BF16)], axis=0)
        acc_sc[t % 2] = jnp.dot(vmt_ref[...], p_pad, preferred_element_type=F32)

    def live_keys(t, blk, c):
        return min(tk, c % tq + COL_CHUNK) if t == blk else tk

    def unit_a(slot, t, blk):
        k_blk = k_ref[blk * tk:(blk + 1) * tk, :]
        k_blk = jnp.concatenate([k_blk, k_blk], axis=1)
        for c in range(0, 2 * tq, COL_CHUNK):
            cols = slice(c, c + COL_CHUNK)
            n_keys = live_keys(t, blk, c)
            s = jnp.dot(k_blk[:n_keys], qs_sc[t % 2, :, cols], preferred_element_type=F32)
            if t - blk >= 2:
                shift = far
            else:
                s = s + bd_ref[t - blk, :n_keys, c % tq:c % tq + COL_CHUNK]
                shift = 0.0
            s_sc[slot, :n_keys, cols] = s
            mx_sc[slot, :, cols] = jnp.max(s, axis=0, keepdims=True) + shift

    def unit_b(slot, t, blk):
        shift = far if t - blk >= 2 else 0.0
        for c in range(0, 2 * tq, COL_CHUNK):
            cols = slice(c, c + COL_CHUNK)
            n_keys = live_keys(t, blk, c)
            m_prev = m_sc[t % 2, :, cols]
            m_new = jnp.maximum(m_prev, mx_sc[slot, :, cols])
            alpha = jnp.exp2(m_prev - m_new)
            p = jnp.exp2(s_sc[slot, :n_keys, cols] - (m_new - shift))
            acc_sc[t % 2, :, cols] = alpha * acc_sc[t % 2, :, cols] + jnp.dot(
                vt_sc[blk, :, :n_keys], p.astype(BF16), preferred_element_type=F32)
            m_sc[t % 2, :, cols] = m_new

    def epilogue(t):
        lam = (jnp.exp(jnp.sum(lam_ref[0:1, :] * lam_ref[1:2, :], axis=1, keepdims=True))
               - jnp.exp(jnp.sum(lam_ref[2:3, :] * lam_ref[3:4, :], axis=1, keepdims=True))
               + lam_init)
        o = acc_sc[t % 2, :V_DIM, :] / acc_sc[t % 2, V_DIM:V_DIM + 1, :]
        o = o[:, :tq] - lam * o[:, tq:]
        var = jnp.mean(o * o, axis=0, keepdims=True)
        o = o * lax.rsqrt(var + NORM_EPS) * sw_ref[...] * (1.0 - lam_init)
        o_ref[t * tq:(t + 1) * tq, :] = o.T.astype(o_ref.dtype)

    units = [(t, blk) for t in range(n_q) for blk in range(t + 1)]
    stages = [{"epilogue": [], "prologue": [], "a": [], "b": []} for _ in range(len(units) + 2)]
    for k, (t, blk) in enumerate(units):
        stages[k]["a"].append(functools.partial(unit_a, k % 2, t, blk))
        stages[k + 1]["b"].append(functools.partial(unit_b, k % 2, t, blk))
        if blk == 0:
            stages[max(k - 1, 0)]["prologue"].append(functools.partial(prologue, t))
        if blk == t:
            stages[k + 2]["epilogue"].append(functools.partial(epilogue, t))
    stages[0]["epilogue"].append(transpose_values)

    for first in range(0, len(stages), STAGES_PER_REGION):
        @pl.when(pl.program_id(0) >= -first)
        def _(group=stages[first:first + STAGES_PER_REGION]):
            for stage in group:
                for part in ("epilogue", "prologue", "a", "b"):
                    for fn in stage[part]:
                        fn()


def _diff_attn(q, k, v, k_meta, v_meta_t, bias_diag, bias_meta, far_bias, lam_vecs, subln_w,
               *, batch, seq, lam_init):
    n_meta = k_meta.shape[0]
    seq_spec = pl.BlockSpec((seq, V_DIM), lambda b, h: (b, h))
    return pl.pallas_call(
        functools.partial(_attn_kernel, lam_init=lam_init),
        grid=(batch, N_HEADS),
        in_specs=[
            pl.BlockSpec(memory_space=pltpu.SMEM),
            seq_spec, seq_spec, seq_spec,
            pl.BlockSpec((n_meta, V_DIM), lambda b, h: (0, h)),
            pl.BlockSpec((V_ROWS, LANES), lambda b, h: (h, 0)),
            pl.BlockSpec((None, 2, ATTN_K, ATTN_Q), lambda b, h: (h, 0, 0, 0)),
            pl.BlockSpec((None, 2, n_meta, ATTN_Q), lambda b, h: (h, 0, 0, 0)),
            pl.BlockSpec((4, HEAD_DIM), lambda b, h: (0, 0)),
            pl.BlockSpec((V_DIM, 1), lambda b, h: (0, 0)),
        ],
        out_specs=seq_spec,
        out_shape=jax.ShapeDtypeStruct(q.shape, BF16),
        scratch_shapes=[
            pltpu.VMEM((seq // ATTN_K, V_ROWS, ATTN_K), BF16),
            pltpu.VMEM((2, 2 * V_DIM, 2 * ATTN_Q), BF16),
            pltpu.VMEM((2, ATTN_K, 2 * ATTN_Q), F32),
            pltpu.VMEM((2, 1, 2 * ATTN_Q), F32),
            pltpu.VMEM((2, 1, 2 * ATTN_Q), F32),
            pltpu.VMEM((2, V_ROWS, 2 * ATTN_Q), F32),
        ],
        compiler_params=pltpu.CompilerParams(
            dimension_semantics=("arbitrary", "arbitrary"),
            vmem_limit_bytes=VMEM_LIMIT),
        name="diff_attn",
    )(far_bias, q, k, v, k_meta, v_meta_t, bias_diag, bias_meta, lam_vecs, subln_w)


def _merge_ffn_kernel(u_ref, uprev_ref, umeta_ref, x_ref, o_ref, gp_ref, ga_ref,
                      gw_ref, ps_ref, wpo_ref, wao_ref, wo_ref, fnw_ref,
                      wg_ref, wu_ref, wd_ref, onw_ref, out_ref, ext_sc, hn_sc, acc_sc,
                      *, tiles_per_seq):
    rows = u_ref.shape[0]
    gdim = gw_ref.shape[1]
    hist = uprev_ref.shape[0]

    first = pl.program_id(0) % tiles_per_seq == 0
    ext_sc[0:hist, :] = jnp.where(first, umeta_ref[...], uprev_ref[...])
    ext_sc[hist:, :] = u_ref[...]

    pool_parts = []
    for g, win in enumerate(POOL_WINDOWS):
        cols = slice(g * gdim, (g + 1) * gdim)
        total = ext_sc[hist:hist + rows, cols]
        for back in range(1, win):
            total = total + ext_sc[hist - back:hist - back + rows, cols]
        pooled = total / float(win) - ext_sc[hist:hist + rows, cols]
        pool_parts.append(jnp.dot(pooled.astype(BF16), gw_ref[g], preferred_element_type=F32))
    pool_out = jnp.concatenate(pool_parts, axis=1) * ps_ref[...]

    merged = (gp_ref[...] * jnp.dot(pool_out.astype(BF16), wpo_ref[...],
                                    preferred_element_type=F32)
              + ga_ref[...] * jnp.dot(o_ref[...], wao_ref[...], preferred_element_type=F32))
    h1 = x_ref[...] + jnp.dot(merged.astype(BF16), wo_ref[...], preferred_element_type=F32)

    hn_sc[...] = _rms(h1, fnw_ref[...]).astype(BF16)
    acc_sc[...] = h1

    for c in range(0, wg_ref.shape[1], FFN_COLS):
        hn = hn_sc[...]
        gate = jnp.dot(hn, wg_ref[:, c:c + FFN_COLS], preferred_element_type=F32)
        up = jnp.dot(hn, wu_ref[:, c:c + FFN_COLS], preferred_element_type=F32)
        act = (jax.nn.silu(gate) * up).astype(BF16)
        acc_sc[...] += jnp.dot(act, wd_ref[c:c + FFN_COLS, :], preferred_element_type=F32)
    out_ref[...] = _rms(acc_sc[...], onw_ref[...])


def _merge_ffn(u, u_meta, x2d, o, gp, ga, weights, *, seq):
    m, d_model = x2d.shape
    pool_w = u.shape[1]
    hist = u_meta.shape[0]
    rows = MERGE_ROWS
    per_tile = rows // hist
    row_spec = lambda width: pl.BlockSpec((rows, width), lambda i: (i, 0))
    prev_spec = pl.BlockSpec((hist, pool_w), lambda i: (jnp.maximum(i * per_tile - 1, 0), 0))
    return pl.pallas_call(
        functools.partial(_merge_ffn_kernel, tiles_per_seq=seq // rows),
        grid=(m // rows,),
        in_specs=[row_spec(pool_w), prev_spec, _const_spec(u_meta.shape),
                  row_spec(d_model), row_spec(o.shape[1]), row_spec(d_model), row_spec(d_model)]
                 + [_const_spec(w.shape) for w in weights],
        out_specs=row_spec(d_model),
        out_shape=jax.ShapeDtypeStruct((m, d_model), F32),
        scratch_shapes=[pltpu.VMEM((hist + rows, pool_w), F32),
                        pltpu.VMEM((rows, d_model), BF16),
                        pltpu.VMEM((rows, d_model), F32)],
        compiler_params=pltpu.CompilerParams(
            dimension_semantics=("arbitrary",), vmem_limit_bytes=VMEM_LIMIT),
        name="merge_ffn",
    )(u, u, u_meta, x2d, o, gp, ga, *weights)


def _t5_bucket(rel):
    nb = N_BUCKETS // 2
    ret = jnp.where(rel > 0, nb, 0)
    n = jnp.abs(rel)
    max_exact = nb // 2
    nf = jnp.maximum(n, max_exact).astype(jnp.float32)
    large = max_exact + (jnp.log(nf / max_exact) / math.log(MAX_DISTANCE / max_exact)
                         * (nb - max_exact)).astype(jnp.int32)
    large = jnp.minimum(large, nb - 1)
    return ret + jnp.where(n < max_exact, n, large)


def _bias_kernel(w_ref, diag_ref, meta_ref):
    _, tk, tq = diag_ref.shape
    n_meta = meta_ref.shape[1]
    period = w_ref.shape[1]

    def toeplitz(row, n_rows):
        x = jnp.broadcast_to(w_ref[row:row + 1, :], (n_rows, period))
        return pltpu.roll(x, 0, 1, stride=1, stride_axis=0)[:, :tq]

    key = lax.broadcasted_iota(jnp.int32, (tk, tq), 0)
    qry = lax.broadcasted_iota(jnp.int32, (tk, tq), 1)
    diag_ref[0] = jnp.where(key // CHUNK <= qry // CHUNK, toeplitz(0, tk), NEG_INF)
    diag_ref[1] = toeplitz(1, tk)
    meta_ref[0] = toeplitz(2, n_meta)
    meta_ref[1] = jnp.broadcast_to(w_ref[3:4, :tq], (n_meta, tq))


def _bias_tables(rel_bias_table):
    tq, tk = ATTN_Q, ATTN_K
    n_heads = rel_bias_table.shape[1]
    assert tq == tk and tk % CHUNK == 0 and tk + 1 >= MAX_DISTANCE
    period = 2 * tk
    signed = np.where(np.arange(period) < tk, np.arange(period), np.arange(period) - period)
    rel = np.stack([
        -signed,
        -signed - tk,
        -signed - N_META,
        np.full(period, -(tk + N_META)),
    ])
    vectors = rel_bias_table[_t5_bucket(jnp.asarray(rel))].astype(F32) * LOG2_E
    vectors = jnp.transpose(vectors, (2, 0, 1))
    diag, meta = pl.pallas_call(
        _bias_kernel,
        grid=(n_heads,),
        in_specs=[pl.BlockSpec((None, 4, period), lambda h: (h, 0, 0))],
        out_specs=(pl.BlockSpec((None, 2, tk, tq), lambda h: (h, 0, 0, 0)),
                   pl.BlockSpec((None, 2, N_META, tq), lambda h: (h, 0, 0, 0))),
        out_shape=(jax.ShapeDtypeStruct((n_heads, 2, tk, tq), F32),
                   jax.ShapeDtypeStruct((n_heads, 2, N_META, tq), F32)),
        compiler_params=pltpu.CompilerParams(dimension_semantics=("arbitrary",)),
        name="bias_tiles",
    )(vectors)
    return diag, meta, vectors[:, 3, 0]


def kernel(x, meta_tokens, rel_bias_table, mix_norm_w, w_in, pool_group_w, pool_scale,
           lambda_q1, lambda_k1, lambda_q2, lambda_k2, subln_w, w_pool_out, w_attn_out,
           w_o, ffn_norm_w, w_gate, w_up, w_down, final_norm_w):
    batch, seq, d_model = x.shape
    assert w_in.shape[0] == 1, "single-layer block"
    pool_w = pool_scale.shape[1]
    attn_w = N_HEADS * V_DIM
    assert seq % ATTN_Q == 0 and seq % MERGE_ROWS == 0 and (batch * seq) % PROJ_ROWS == 0
    assert N_META >= max(POOL_WINDOWS) and MERGE_ROWS % N_META == 0

    w_in_bf = w_in[0].astype(BF16)
    x2d = x.reshape(batch * seq, d_model)
    u, q, k, v, gp, ga = _in_proj(x2d, mix_norm_w[0], w_in_bf, rows=PROJ_ROWS,
                                  pool_w=pool_w, attn_w=attn_w)
    u_meta, _, k_meta, v_meta, _, _ = _in_proj(meta_tokens.astype(x.dtype), mix_norm_w[0], w_in_bf,
                                               rows=N_META, pool_w=pool_w, attn_w=attn_w)
    v_meta_t = v_meta.T.reshape(N_HEADS, V_DIM, N_META)
    ones_rows = jnp.zeros((N_HEADS, BF16_ROWS, N_META), BF16).at[:, 0, :].set(1)
    v_meta_t = jnp.pad(jnp.concatenate([v_meta_t, ones_rows], axis=1),
                       ((0, 0), (0, 0), (0, LANES - N_META))).reshape(N_HEADS * V_ROWS, LANES)

    bias_diag, bias_meta, far_bias = _bias_tables(rel_bias_table)
    lam_init = 0.8 - 0.6 * math.exp(-0.3 * LAYER_IDX)
    lam_vecs = jnp.stack([lambda_q1[0], lambda_k1[0], lambda_q2[0], lambda_k2[0]]).astype(F32)
    o = _diff_attn(q, k, v, k_meta, v_meta_t, bias_diag, bias_meta, far_bias, lam_vecs,
                   subln_w[0].reshape(V_DIM, 1).astype(F32),
                   batch=batch, seq=seq, lam_init=lam_init)

    assert w_gate.shape[2] % FFN_COLS == 0
    weights = (
        pool_group_w[0].astype(BF16), pool_scale[0].reshape(1, pool_w),
        w_pool_out[0].astype(BF16), w_attn_out[0].astype(BF16), w_o[0].astype(BF16),
        ffn_norm_w[0].reshape(1, d_model),
        w_gate[0].astype(BF16), w_up[0].astype(BF16), w_down[0].astype(BF16),
        final_norm_w.reshape(1, d_model))
    out = _merge_ffn(u, u_meta, x2d, o, gp, ga, weights, seq=seq)
    return out.reshape(batch, seq, d_model)
```

```python
import functools
import math

import jax
import jax.numpy as jnp
import numpy as np
from jax import lax
from jax.experimental import pallas as pl
from jax.experimental.pallas import tpu as pltpu

CHUNK = 64
N_META = 16
POOL_WINDOWS = (2, 4, 8, 16)
N_HEADS = 8
HEAD_DIM = 64
N_BUCKETS = 32
MAX_DISTANCE = 128
NORM_EPS = 1e-6
NEG_INF = -1e30
LOG2_E = math.log2(math.e)
LAYER_IDX = 0

V_DIM = 2 * HEAD_DIM
LANES = 128
BF16_ROWS = 16
V_ROWS = V_DIM + BF16_ROWS

PROJ_ROWS = 512
ATTN_Q = 512
ATTN_K = 512
STAGES_PER_REGION = 19
COL_CHUNK = 256
MERGE_ROWS = 512
FFN_COLS = 256
VMEM_LIMIT = 56 * 1024 * 1024

BF16 = jnp.bfloat16
F32 = jnp.float32


def _const_spec(shape):
    return pl.BlockSpec(shape, lambda *_: (0,) * len(shape), pipeline_mode=pl.Buffered(1))


def _rms(x, w):
    var = jnp.mean(x * x, axis=-1, keepdims=True)
    return x * lax.rsqrt(var + NORM_EPS) * w


def _in_proj_kernel(x_ref, nw_ref, w_ref, u_ref, q_ref, k_ref, v_ref, gp_ref, ga_ref,
                    *, pool_w, attn_w, d_model):
    xn = _rms(x_ref[...], nw_ref[...]).astype(BF16)

    def proj(lo, width):
        return jnp.dot(xn, w_ref[:, lo:lo + width], preferred_element_type=F32)

    lo = 0
    u_ref[...] = proj(lo, pool_w)
    lo += pool_w
    q_ref[...] = (proj(lo, attn_w) * (HEAD_DIM ** -0.5)).astype(BF16)
    lo += attn_w
    k_ref[...] = proj(lo, attn_w).astype(BF16)
    lo += attn_w
    v_ref[...] = proj(lo, attn_w).astype(BF16)
    lo += attn_w
    gp_ref[...] = jax.nn.sigmoid(proj(lo, d_model))
    lo += d_model
    ga_ref[...] = jax.nn.sigmoid(proj(lo, d_model))


def _in_proj(x2d, norm_w, w_in_bf, *, rows, pool_w, attn_w):
    m, d_model = x2d.shape
    in_cols = w_in_bf.shape[1]
    row_spec = lambda width: pl.BlockSpec((rows, width), lambda i: (i, 0))
    out_shape = (
        jax.ShapeDtypeStruct((m, pool_w), F32),
        jax.ShapeDtypeStruct((m, attn_w), BF16),
        jax.ShapeDtypeStruct((m, attn_w), BF16),
        jax.ShapeDtypeStruct((m, attn_w), BF16),
        jax.ShapeDtypeStruct((m, d_model), F32),
        jax.ShapeDtypeStruct((m, d_model), F32),
    )
    return pl.pallas_call(
        functools.partial(_in_proj_kernel, pool_w=pool_w, attn_w=attn_w, d_model=d_model),
        grid=(m // rows,),
        in_specs=[row_spec(d_model), _const_spec((1, d_model)), _const_spec((d_model, in_cols))],
        out_specs=tuple(row_spec(s.shape[1]) for s in out_shape),
        out_shape=out_shape,
        compiler_params=pltpu.CompilerParams(
            dimension_semantics=("arbitrary",), vmem_limit_bytes=VMEM_LIMIT),
        name="in_proj",
    )(x2d, norm_w.reshape(1, d_model), w_in_bf)


def _attn_kernel(far_ref, q_ref, k_ref, v_ref, km_ref, vmt_ref, bd_ref, bm_ref, lam_ref, sw_ref,
                 o_ref, vt_sc, qs_sc, s_sc, mx_sc, m_sc, acc_sc, *, lam_init):
    h = pl.program_id(1)
    n_blk, v_rows, tk = vt_sc.shape
    tq = tk
    n_q = q_ref.shape[0] // tq
    far = far_ref[h]

    def transpose_values():
        ones_row = lax.broadcasted_iota(jnp.int32, (v_rows - V_DIM, tk), 0) == 0
        for blk in range(n_blk):
            vt_sc[blk, :V_DIM, :] = v_ref[blk * tk:(blk + 1) * tk, :].astype(F32).T.astype(BF16)
            vt_sc[blk, V_DIM:, :] = ones_row.astype(BF16)

    def prologue(t):
        qt = q_ref[t * tq:(t + 1) * tq, :].astype(F32).T * LOG2_E
        row = lax.broadcasted_iota(jnp.int32, qt.shape, 0)
        zero = jnp.zeros_like(qt)
        qt = jnp.concatenate([jnp.where(row < HEAD_DIM, qt, zero),
                              jnp.where(row >= HEAD_DIM, qt, zero)], axis=1)
        hi = qt.astype(BF16)
        lo = (qt - hi.astype(F32)).astype(BF16)
        qs = jnp.concatenate([hi, lo], axis=0)
        qs_sc[t % 2] = qs
        bm = bm_ref[min(t, 1)]
        km = km_ref[...]
        s = jnp.dot(jnp.concatenate([km, km], axis=1), qs, preferred_element_type=F32)
        s = s + jnp.concatenate([bm, bm], axis=1)
        m0 = jnp.max(s, axis=0, keepdims=True)
        p = jnp.exp2(s - m0)
        m_sc[t % 2] = m0
        p_pad = jnp.concatenate(
            [p.astype(BF16), jnp.zeros((vmt_ref.shape[1] - p.shape[0], p.shape[1]), BF16)], axis=0)
        acc_sc[t % 2] = jnp.dot(vmt_ref[...], p_pad, preferred_element_type=F32)

    def live_keys(t, blk, c):
        return min(tk, c % tq + COL_CHUNK) if t == blk else tk

    def unit_a(slot, t, blk):
        k_blk = k_ref[blk * tk:(blk + 1) * tk, :]
        k_blk = jnp.concatenate([k_blk, k_blk], axis=1)
        for c in range(0, 2 * tq, COL_CHUNK):
            cols = slice(c, c + COL_CHUNK)
            n_keys = live_keys(t, blk, c)
            s = jnp.dot(k_blk[:n_keys], qs_sc[t % 2, :, cols], preferred_element_type=F32)
            if t - blk >= 2:
                shift = far
            else:
                s = s + bd_ref[t - blk, :n_keys, c % tq:c % tq + COL_CHUNK]
                shift = 0.0
            s_sc[slot, :n_keys, cols] = s
            mx_sc[slot, :, cols] = jnp.max(s, axis=0, keepdims=True) + shift

    def unit_b(slot, t, blk):
        shift = far if t - blk >= 2 else 0.0
        for c in range(0, 2 * tq, COL_CHUNK):
            cols = slice(c, c + COL_CHUNK)
            n_keys = live_keys(t, blk, c)
            m_prev = m_sc[t % 2, :, cols]
            m_new = jnp.maximum(m_prev, mx_sc[slot, :, cols])
            alpha = jnp.exp2(m_prev - m_new)
            p = jnp.exp2(s_sc[slot, :n_keys, cols] - (m_new - shift))
            acc_sc[t % 2, :, cols] = alpha * acc_sc[t % 2, :, cols] + jnp.dot(
                vt_sc[blk, :, :n_keys], p.astype(BF16), preferred_element_type=F32)
            m_sc[t % 2, :, cols] = m_new

    def epilogue(t):
        lam = (jnp.exp(jnp.sum(lam_ref[0:1, :] * lam_ref[1:2, :], axis=1, keepdims=True))
               - jnp.exp(jnp.sum(lam_ref[2:3, :] * lam_ref[3:4, :], axis=1, keepdims=True))
               + lam_init)
        o = acc_sc[t % 2, :V_DIM, :] / acc_sc[t % 2, V_DIM:V_DIM + 1, :]
        o = o[:, :tq] - lam * o[:, tq:]
        var = jnp.mean(o * o, axis=0, keepdims=True)
        o = o * lax.rsqrt(var + NORM_EPS) * sw_ref[...] * (1.0 - lam_init)
        o_ref[t * tq:(t + 1) * tq, :] = o.T.astype(o_ref.dtype)

    units = [(t, blk) for t in range(n_q) for blk in range(t + 1)]
    stages = [{"epilogue": [], "prologue": [], "a": [], "b": []} for _ in range(len(units) + 2)]
    for k, (t, blk) in enumerate(units):
        stages[k]["a"].append(functools.partial(unit_a, k % 2, t, blk))
        stages[k + 1]["b"].append(functools.partial(unit_b, k % 2, t, blk))
        if blk == 0:
            stages[max(k - 1, 0)]["prologue"].append(functools.partial(prologue, t))
        if blk == t:
            stages[k + 2]["epilogue"].append(functools.partial(epilogue, t))
    stages[0]["epilogue"].append(transpose_values)

    for first in range(0, len(stages), STAGES_PER_REGION):
        @pl.when(pl.program_id(0) >= -first)
        def _(group=stages[first:first + STAGES_PER_REGION]):
            for stage in group:
                for part in ("epilogue", "prologue", "a", "b"):
                    for fn in stage[part]:
                        fn()


def _diff_attn(q, k, v, k_meta, v_meta_t, bias_diag, bias_meta, far_bias, lam_vecs, subln_w,
               *, batch, seq, lam_init):
    n_meta = k_meta.shape[0]
    seq_spec = pl.BlockSpec((seq, V_DIM), lambda b, h: (b, h))
    return pl.pallas_call(
        functools.partial(_attn_kernel, lam_init=lam_init),
        grid=(batch, N_HEADS),
        in_specs=[
            pl.BlockSpec(memory_space=pltpu.SMEM),
            seq_spec, seq_spec, seq_spec,
            pl.BlockSpec((n_meta, V_DIM), lambda b, h: (0, h)),
            pl.BlockSpec((V_ROWS, LANES), lambda b, h: (h, 0)),
            pl.BlockSpec((None, 2, ATTN_K, ATTN_Q), lambda b, h: (h, 0, 0, 0)),
            pl.BlockSpec((None, 2, n_meta, ATTN_Q), lambda b, h: (h, 0, 0, 0)),
            pl.BlockSpec((4, HEAD_DIM), lambda b, h: (0, 0)),
            pl.BlockSpec((V_DIM, 1), lambda b, h: (0, 0)),
        ],
        out_specs=seq_spec,
        out_shape=jax.ShapeDtypeStruct(q.shape, BF16),
        scratch_shapes=[
            pltpu.VMEM((seq // ATTN_K, V_ROWS, ATTN_K), BF16),
            pltpu.VMEM((2, 2 * V_DIM, 2 * ATTN_Q), BF16),
            pltpu.VMEM((2, ATTN_K, 2 * ATTN_Q), F32),
            pltpu.VMEM((2, 1, 2 * ATTN_Q), F32),
            pltpu.VMEM((2, 1, 2 * ATTN_Q), F32),
            pltpu.VMEM((2, V_ROWS, 2 * ATTN_Q), F32),
        ],
        compiler_params=pltpu.CompilerParams(
            dimension_semantics=("arbitrary", "arbitrary"),
            vmem_limit_bytes=VMEM_LIMIT),
        name="diff_attn",
    )(far_bias, q, k, v, k_meta, v_meta_t, bias_diag, bias_meta, lam_vecs, subln_w)


def _merge_ffn_kernel(u_ref, uprev_ref, umeta_ref, x_ref, o_ref, gp_ref, ga_ref,
                      gw_ref, ps_ref, wpo_ref, wao_ref, wo_ref, fnw_ref,
                      wg_ref, wu_ref, wd_ref, onw_ref, out_ref, ext_sc, hn_sc, acc_sc,
                      *, tiles_per_seq):
    rows = u_ref.shape[0]
    gdim = gw_ref.shape[1]
    hist = uprev_ref.shape[0]

    first = pl.program_id(0) % tiles_per_seq == 0
    ext_sc[0:hist, :] = jnp.where(first, umeta_ref[...], uprev_ref[...])
    ext_sc[hist:, :] = u_ref[...]

    pool_parts = []
    for g, win in enumerate(POOL_WINDOWS):
        cols = slice(g * gdim, (g + 1) * gdim)
        assert win & (win - 1) == 0 and win <= hist
        ext = ext_sc[:, cols]
        total = ext
        span = 1
        while span < win:
            total = total + pltpu.roll(total, span, 0)
            span *= 2
        pooled = total[hist:] / float(win) - ext[hist:]
        pool_parts.append(jnp.dot(pooled.astype(BF16), gw_ref[g], preferred_element_type=F32))
    pool_out = jnp.concatenate(pool_parts, axis=1) * ps_ref[...]

    merged = (gp_ref[...] * jnp.dot(pool_out.astype(BF16), wpo_ref[...],
                                    preferred_element_type=F32)
              + ga_ref[...] * jnp.dot(o_ref[...], wao_ref[...], preferred_element_type=F32))
    h1 = x_ref[...] + jnp.dot(merged.astype(BF16), wo_ref[...], preferred_element_type=F32)

    hn_sc[...] = _rms(h1, fnw_ref[...]).astype(BF16)
    acc_sc[...] = h1

    for c in range(0, wg_ref.shape[1], FFN_COLS):
        hn = hn_sc[...]
        gate = jnp.dot(hn, wg_ref[:, c:c + FFN_COLS], preferred_element_type=F32)
        up = jnp.dot(hn, wu_ref[:, c:c + FFN_COLS], preferred_element_type=F32)
        act = (jax.nn.silu(gate) * up).astype(BF16)
        acc_sc[...] += jnp.dot(act, wd_ref[c:c + FFN_COLS, :], preferred_element_type=F32)
    out_ref[...] = _rms(acc_sc[...], onw_ref[...])


def _merge_ffn(u, u_meta, x2d, o, gp, ga, weights, *, seq):
    m, d_model = x2d.shape
    pool_w = u.shape[1]
    hist = u_meta.shape[0]
    rows = MERGE_ROWS
    per_tile = rows // hist
    row_spec = lambda width: pl.BlockSpec((rows, width), lambda i: (i, 0))
    prev_spec = pl.BlockSpec((hist, pool_w), lambda i: (jnp.maximum(i * per_tile - 1, 0), 0))
    return pl.pallas_call(
        functools.partial(_merge_ffn_kernel, tiles_per_seq=seq // rows),
        grid=(m // rows,),
        in_specs=[row_spec(pool_w), prev_spec, _const_spec(u_meta.shape),
                  row_spec(d_model), row_spec(o.shape[1]), row_spec(d_model), row_spec(d_model)]
                 + [_const_spec(w.shape) for w in weights],
        out_specs=row_spec(d_model),
        out_shape=jax.ShapeDtypeStruct((m, d_model), F32),
        scratch_shapes=[pltpu.VMEM((hist + rows, pool_w), F32),
                        pltpu.VMEM((rows, d_model), BF16),
                        pltpu.VMEM((rows, d_model), F32)],
        compiler_params=pltpu.CompilerParams(
            dimension_semantics=("arbitrary",), vmem_limit_bytes=VMEM_LIMIT),
        name="merge_ffn",
    )(u, u, u_meta, x2d, o, gp, ga, *weights)


def _t5_bucket(rel):
    nb = N_BUCKETS // 2
    ret = jnp.where(rel > 0, nb, 0)
    n = jnp.abs(rel)
    max_exact = nb // 2
    nf = jnp.maximum(n, max_exact).astype(jnp.float32)
    large = max_exact + (jnp.log(nf / max_exact) / math.log(MAX_DISTANCE / max_exact)
                         * (nb - max_exact)).astype(jnp.int32)
    large = jnp.minimum(large, nb - 1)
    return ret + jnp.where(n < max_exact, n, large)


def _bias_kernel(w_ref, diag_ref, meta_ref):
    _, tk, tq = diag_ref.shape
    n_meta = meta_ref.shape[1]
    period = w_ref.shape[1]

    def toeplitz(row, n_rows):
        x = jnp.broadcast_to(w_ref[row:row + 1, :], (n_rows, period))
        return pltpu.roll(x, 0, 1, stride=1, stride_axis=0)[:, :tq]

    key = lax.broadcasted_iota(jnp.int32, (tk, tq), 0)
    qry = lax.broadcasted_iota(jnp.int32, (tk, tq), 1)
    diag_ref[0] = jnp.where(key // CHUNK <= qry // CHUNK, toeplitz(0, tk), NEG_INF)
    diag_ref[1] = toeplitz(1, tk)
    meta_ref[0] = toeplitz(2, n_meta)
    meta_ref[1] = jnp.broadcast_to(w_ref[3:4, :tq], (n_meta, tq))


def _bias_tables(rel_bias_table):
    tq, tk = ATTN_Q, ATTN_K
    n_heads = rel_bias_table.shape[1]
    assert tq == tk and tk % CHUNK == 0 and tk + 1 >= MAX_DISTANCE
    period = 2 * tk
    signed = np.where(np.arange(period) < tk, np.arange(period), np.arange(period) - period)
    rel = np.stack([
        -signed,
        -signed - tk,
        -signed - N_META,
        np.full(period, -(tk + N_META)),
    ])
    vectors = rel_bias_table[_t5_bucket(jnp.asarray(rel))].astype(F32) * LOG2_E
    vectors = jnp.transpose(vectors, (2, 0, 1))
    diag, meta = pl.pallas_call(
        _bias_kernel,
        grid=(n_heads,),
        in_specs=[pl.BlockSpec((None, 4, period), lambda h: (h, 0, 0))],
        out_specs=(pl.BlockSpec((None, 2, tk, tq), lambda h: (h, 0, 0, 0)),
                   pl.BlockSpec((None, 2, N_META, tq), lambda h: (h, 0, 0, 0))),
        out_shape=(jax.ShapeDtypeStruct((n_heads, 2, tk, tq), F32),
                   jax.ShapeDtypeStruct((n_heads, 2, N_META, tq), F32)),
        compiler_params=pltpu.CompilerParams(dimension_semantics=("arbitrary",)),
        name="bias_tiles",
    )(vectors)
    return diag, meta, vectors[:, 3, 0]


def kernel(x, meta_tokens, rel_bias_table, mix_norm_w, w_in, pool_group_w, pool_scale,
           lambda_q1, lambda_k1, lambda_q2, lambda_k2, subln_w, w_pool_out, w_attn_out,
           w_o, ffn_norm_w, w_gate, w_up, w_down, final_norm_w):
    batch, seq, d_model = x.shape
    assert w_in.shape[0] == 1, "single-layer block"
    pool_w = pool_scale.shape[1]
    attn_w = N_HEADS * V_DIM
    assert seq % ATTN_Q == 0 and seq % MERGE_ROWS == 0 and (batch * seq) % PROJ_ROWS == 0
    assert N_META >= max(POOL_WINDOWS) and MERGE_ROWS % N_META == 0

    w_in_bf = w_in[0].astype(BF16)
    x2d = x.reshape(batch * seq, d_model)
    u, q, k, v, gp, ga = _in_proj(x2d, mix_norm_w[0], w_in_bf, rows=PROJ_ROWS,
                                  pool_w=pool_w, attn_w=attn_w)
    u_meta, _, k_meta, v_meta, _, _ = _in_proj(meta_tokens.astype(x.dtype), mix_norm_w[0], w_in_bf,
                                               rows=N_META, pool_w=pool_w, attn_w=attn_w)
    v_meta_t = v_meta.T.reshape(N_HEADS, V_DIM, N_META)
    ones_rows = jnp.zeros((N_HEADS, BF16_ROWS, N_META), BF16).at[:, 0, :].set(1)
    v_meta_t = jnp.pad(jnp.concatenate([v_meta_t, ones_rows], axis=1),
                       ((0, 0), (0, 0), (0, LANES - N_META))).reshape(N_HEADS * V_ROWS, LANES)

    bias_diag, bias_meta, far_bias = _bias_tables(rel_bias_table)
    lam_init = 0.8 - 0.6 * math.exp(-0.3 * LAYER_IDX)
    lam_vecs = jnp.stack([lambda_q1[0], lambda_k1[0], lambda_q2[0], lambda_k2[0]]).astype(F32)
    o = _diff_attn(q, k, v, k_meta, v_meta_t, bias_diag, bias_meta, far_bias, lam_vecs,
                   subln_w[0].reshape(V_DIM, 1).astype(F32),
                   batch=batch, seq=seq, lam_init=lam_init)

    assert w_gate.shape[2] % FFN_COLS == 0
    weights = (
        pool_group_w[0].astype(BF16), pool_scale[0].reshape(1, pool_w),
        w_pool_out[0].astype(BF16), w_attn_out[0].astype(BF16), w_o[0].astype(BF16),
        ffn_norm_w[0].reshape(1, d_model),
        w_gate[0].astype(BF16), w_up[0].astype(BF16), w_down[0].astype(BF16),
        final_norm_w.reshape(1, d_model))
    out = _merge_ffn(u, u_meta, x2d, o, gp, ga, weights, seq=seq)
    return out.reshape(batch, seq, d_model)
```

```python
import functools
import itertools
import math

import jax
import jax.numpy as jnp
import numpy as np
from jax import lax
from jax.experimental import pallas as pl
from jax.experimental.pallas import tpu as pltpu

CHUNK = 64
N_META = 16
POOL_WINDOWS = (2, 4, 8, 16)
N_HEADS = 8
HEAD_DIM = 64
N_BUCKETS = 32
MAX_DISTANCE = 128
NORM_EPS = 1e-6
NEG_INF = -1e30
LOG2_E = math.log2(math.e)
LAYER_IDX = 0

V_DIM = 2 * HEAD_DIM
LANES = 128
BF16_ROWS = 16
V_ROWS = V_DIM + BF16_ROWS

PROJ_ROWS = 512
ATTN_Q = 512
ATTN_K = 512
STAGES_PER_REGION = 13
SCORE_SLOTS = 2
COL_CHUNK = 256
MERGE_ROWS = 512
FFN_COLS = 256
VMEM_LIMIT = 56 * 1024 * 1024

BF16 = jnp.bfloat16
F32 = jnp.float32


def _const_spec(shape):
    return pl.BlockSpec(shape, lambda *_: (0,) * len(shape), pipeline_mode=pl.Buffered(1))


def _rms(x, w):
    var = jnp.mean(x * x, axis=-1, keepdims=True)
    return x * lax.rsqrt(var + NORM_EPS) * w


def _in_proj_kernel(x_ref, nw_ref, w_ref, u_ref, q_ref, k_ref, v_ref, gp_ref, ga_ref,
                    *, pool_w, attn_w, d_model):
    xn = _rms(x_ref[...], nw_ref[...]).astype(BF16)

    def proj(lo, width):
        return jnp.dot(xn, w_ref[:, lo:lo + width], preferred_element_type=F32)

    lo = 0
    u_ref[...] = proj(lo, pool_w)
    lo += pool_w
    q_ref[...] = (proj(lo, attn_w) * (HEAD_DIM ** -0.5)).astype(BF16)
    lo += attn_w
    k_ref[...] = proj(lo, attn_w).astype(BF16)
    lo += attn_w
    v_ref[...] = proj(lo, attn_w).astype(BF16)
    lo += attn_w
    gp_ref[...] = jax.nn.sigmoid(proj(lo, d_model))
    lo += d_model
    ga_ref[...] = jax.nn.sigmoid(proj(lo, d_model))


def _in_proj(x2d, norm_w, w_in_bf, *, rows, pool_w, attn_w):
    m, d_model = x2d.shape
    in_cols = w_in_bf.shape[1]
    row_spec = lambda width: pl.BlockSpec((rows, width), lambda i: (i, 0))
    out_shape = (
        jax.ShapeDtypeStruct((m, pool_w), F32),
        jax.ShapeDtypeStruct((m, attn_w), BF16),
        jax.ShapeDtypeStruct((m, attn_w), BF16),
        jax.ShapeDtypeStruct((m, attn_w), BF16),
        jax.ShapeDtypeStruct((m, d_model), F32),
        jax.ShapeDtypeStruct((m, d_model), F32),
    )
    return pl.pallas_call(
        functools.partial(_in_proj_kernel, pool_w=pool_w, attn_w=attn_w, d_model=d_model),
        grid=(m // rows,),
        in_specs=[row_spec(d_model), _const_spec((1, d_model)), _const_spec((d_model, in_cols))],
        out_specs=tuple(row_spec(s.shape[1]) for s in out_shape),
        out_shape=out_shape,
        compiler_params=pltpu.CompilerParams(
            dimension_semantics=("arbitrary",), vmem_limit_bytes=VMEM_LIMIT),
        name="in_proj",
    )(x2d, norm_w.reshape(1, d_model), w_in_bf)


def _attn_kernel(far_ref, q_ref, k_ref, v_ref, km_ref, vmt_ref, bd_ref, bm_ref, lam_ref, sw_ref,
                 o_ref, vt_sc, qs_sc, s_sc, mx_sc, m_sc, acc_sc, *, lam_init):
    h = pl.program_id(1)
    n_blk, v_rows, tk = vt_sc.shape
    tq = tk
    n_q = q_ref.shape[0] // tq
    far = far_ref[h]

    def transpose_values():
        ones_row = lax.broadcasted_iota(jnp.int32, (v_rows - V_DIM, tk), 0) == 0
        for blk in range(n_blk):
            vt_sc[blk, :V_DIM, :] = v_ref[blk * tk:(blk + 1) * tk, :].astype(F32).T.astype(BF16)
            vt_sc[blk, V_DIM:, :] = ones_row.astype(BF16)

    def prologue(t):
        qt = q_ref[t * tq:(t + 1) * tq, :].astype(F32).T * LOG2_E
        row = lax.broadcasted_iota(jnp.int32, qt.shape, 0)
        zero = jnp.zeros_like(qt)
        qt = jnp.concatenate([jnp.where(row < HEAD_DIM, qt, zero),
                              jnp.where(row >= HEAD_DIM, qt, zero)], axis=1)
        hi = qt.astype(BF16)
        lo = (qt - hi.astype(F32)).astype(BF16)
        qs = jnp.concatenate([hi, lo], axis=0)
        qs_sc[t % 2] = qs
        bm = bm_ref[min(t, 1)]
        km = km_ref[...]
        s = jnp.dot(jnp.concatenate([km, km], axis=1), qs, preferred_element_type=F32)
        s = s + jnp.concatenate([bm, bm], axis=1)
        m0 = jnp.max(s, axis=0, keepdims=True)
        p = jnp.exp2(s - m0)
        m_sc[t % 2] = m0
        p_pad = jnp.concatenate(
            [p.astype(BF16), jnp.zeros((vmt_ref.shape[1] - p.shape[0], p.shape[1]), BF16)], axis=0)
        acc_sc[t % 2] = jnp.dot(vmt_ref[...], p_pad, preferred_element_type=F32)

    def live_keys(t, blk, c):
        return min(tk, c % tq + COL_CHUNK) if t == blk else tk

    def unit_a(slot, t, blk, c):
        cols = slice(c, c + COL_CHUNK)
        n_keys = live_keys(t, blk, c)
        k_blk = k_ref[blk * tk:blk * tk + n_keys, :]
        s = jnp.dot(jnp.concatenate([k_blk, k_blk], axis=1), qs_sc[t % 2, :, cols],
                    preferred_element_type=F32)
        if t - blk >= 2:
            shift = far
        else:
            s = s + bd_ref[t - blk, :n_keys, c % tq:c % tq + COL_CHUNK]
            shift = 0.0
        s_sc[slot, :n_keys, cols] = s
        mx_sc[slot, :, cols] = jnp.max(s, axis=0, keepdims=True) + shift

    def unit_b(slot, t, blk, c):
        shift = far if t - blk >= 2 else 0.0
        cols = slice(c, c + COL_CHUNK)
        n_keys = live_keys(t, blk, c)
        m_prev = m_sc[t % 2, :, cols]
        m_new = jnp.maximum(m_prev, mx_sc[slot, :, cols])
        alpha = jnp.exp2(m_prev - m_new)
        p = jnp.exp2(s_sc[slot, :n_keys, cols] - (m_new - shift))
        acc_sc[t % 2, :, cols] = alpha * acc_sc[t % 2, :, cols] + jnp.dot(
            vt_sc[blk, :, :n_keys], p.astype(BF16), preferred_element_type=F32)
        m_sc[t % 2, :, cols] = m_new

    def epilogue(t):
        lam = (jnp.exp(jnp.sum(lam_ref[0:1, :] * lam_ref[1:2, :], axis=1, keepdims=True))
               - jnp.exp(jnp.sum(lam_ref[2:3, :] * lam_ref[3:4, :], axis=1, keepdims=True))
               + lam_init)
        o = acc_sc[t % 2, :V_DIM, :] / acc_sc[t % 2, V_DIM:V_DIM + 1, :]
        o = o[:, :tq] - lam * o[:, tq:]
        var = jnp.mean(o * o, axis=0, keepdims=True)
        o = o * lax.rsqrt(var + NORM_EPS) * sw_ref[...] * (1.0 - lam_init)
        o_ref[t * tq:(t + 1) * tq, :] = o.T.astype(o_ref.dtype)

    units = [(t, blk) for t in range(n_q) for blk in range(t + 1)]
    stages = [{"epilogue": [], "prologue": [], "a": [], "b": []} for _ in range(len(units) + 2)]
    for k, (t, blk) in enumerate(units):
        for c in range(0, 2 * tq, COL_CHUNK):
            stages[k]["a"].append(functools.partial(unit_a, k % SCORE_SLOTS, t, blk, c))
            stages[k + 1]["b"].append(functools.partial(unit_b, k % SCORE_SLOTS, t, blk, c))
        if blk == 0:
            stages[max(k - 1, 0)]["prologue"].append(functools.partial(prologue, t))
        if blk == t:
            stages[k + 2]["epilogue"].append(functools.partial(epilogue, t))
    stages[0]["epilogue"].append(transpose_values)

    for first in range(0, len(stages), STAGES_PER_REGION):
        @pl.when(pl.program_id(0) >= -first)
        def _(group=stages[first:first + STAGES_PER_REGION]):
            for stage in group:
                for part in ("epilogue", "prologue"):
                    for fn in stage[part]:
                        fn()
                for pair in itertools.zip_longest(stage["a"], stage["b"]):
                    for fn in pair:
                        if fn is not None:
                            fn()


def _diff_attn(q, k, v, k_meta, v_meta_t, bias_diag, bias_meta, far_bias, lam_vecs, subln_w,
               *, batch, seq, lam_init):
    n_meta = k_meta.shape[0]
    seq_spec = pl.BlockSpec((seq, V_DIM), lambda b, h: (b, h))
    return pl.pallas_call(
        functools.partial(_attn_kernel, lam_init=lam_init),
        grid=(batch, N_HEADS),
        in_specs=[
            pl.BlockSpec(memory_space=pltpu.SMEM),
            seq_spec, seq_spec, seq_spec,
            pl.BlockSpec((n_meta, V_DIM), lambda b, h: (0, h)),
            pl.BlockSpec((V_ROWS, LANES), lambda b, h: (h, 0)),
            pl.BlockSpec((None, 2, ATTN_K, ATTN_Q), lambda b, h: (h, 0, 0, 0)),
            pl.BlockSpec((None, 2, n_meta, ATTN_Q), lambda b, h: (h, 0, 0, 0)),
            pl.BlockSpec((4, HEAD_DIM), lambda b, h: (0, 0)),
            pl.BlockSpec((V_DIM, 1), lambda b, h: (0, 0)),
        ],
        out_specs=seq_spec,
        out_shape=jax.ShapeDtypeStruct(q.shape, BF16),
        scratch_shapes=[
            pltpu.VMEM((seq // ATTN_K, V_ROWS, ATTN_K), BF16),
            pltpu.VMEM((2, 2 * V_DIM, 2 * ATTN_Q), BF16),
            pltpu.VMEM((SCORE_SLOTS, ATTN_K, 2 * ATTN_Q), F32),
            pltpu.VMEM((SCORE_SLOTS, 1, 2 * ATTN_Q), F32),
            pltpu.VMEM((2, 1, 2 * ATTN_Q), F32),
            pltpu.VMEM((2, V_ROWS, 2 * ATTN_Q), F32),
        ],
        compiler_params=pltpu.CompilerParams(
            dimension_semantics=("arbitrary", "arbitrary"),
            vmem_limit_bytes=VMEM_LIMIT),
        name="diff_attn",
    )(far_bias, q, k, v, k_meta, v_meta_t, bias_diag, bias_meta, lam_vecs, subln_w)


def _merge_ffn_kernel(u_ref, uprev_ref, umeta_ref, x_ref, o_ref, gp_ref, ga_ref,
                      gw_ref, ps_ref, wpo_ref, wao_ref, wo_ref, fnw_ref,
                      wg_ref, wu_ref, wd_ref, onw_ref, out_ref, ext_sc, hn_sc, acc_sc,
                      *, tiles_per_seq):
    rows = u_ref.shape[0]
    gdim = gw_ref.shape[1]
    hist = uprev_ref.shape[0]

    first = pl.program_id(0) % tiles_per_seq == 0
    ext_sc[0:hist, :] = jnp.where(first, umeta_ref[...], uprev_ref[...])
    ext_sc[hist:, :] = u_ref[...]

    pool_parts = []
    for g, win in enumerate(POOL_WINDOWS):
        cols = slice(g * gdim, (g + 1) * gdim)
        assert win & (win - 1) == 0 and win <= hist
        ext = ext_sc[:, cols]
        total = ext
        span = 1
        while span < win:
            total = total + pltpu.roll(total, span, 0)
            span *= 2
        pooled = total[hist:] / float(win) - ext[hist:]
        pool_parts.append(jnp.dot(pooled.astype(BF16), gw_ref[g], preferred_element_type=F32))
    pool_out = jnp.concatenate(pool_parts, axis=1) * ps_ref[...]

    merged = (gp_ref[...] * jnp.dot(pool_out.astype(BF16), wpo_ref[...],
                                    preferred_element_type=F32)
              + ga_ref[...] * jnp.dot(o_ref[...], wao_ref[...], preferred_element_type=F32))
    h1 = x_ref[...] + jnp.dot(merged.astype(BF16), wo_ref[...], preferred_element_type=F32)

    hn_sc[...] = _rms(h1, fnw_ref[...]).astype(BF16)
    acc_sc[...] = h1

    for c in range(0, wg_ref.shape[1], FFN_COLS):
        hn = hn_sc[...]
        gate = jnp.dot(hn, wg_ref[:, c:c + FFN_COLS], preferred_element_type=F32)
        up = jnp.dot(hn, wu_ref[:, c:c + FFN_COLS], preferred_element_type=F32)
        act = (jax.nn.silu(gate) * up).astype(BF16)
        acc_sc[...] += jnp.dot(act, wd_ref[c:c + FFN_COLS, :], preferred_element_type=F32)
    out_ref[...] = _rms(acc_sc[...], onw_ref[...])


def _merge_ffn(u, u_meta, x2d, o, gp, ga, weights, *, seq):
    m, d_model = x2d.shape
    pool_w = u.shape[1]
    hist = u_meta.shape[0]
    rows = MERGE_ROWS
    per_tile = rows // hist
    row_spec = lambda width: pl.BlockSpec((rows, width), lambda i: (i, 0))
    prev_spec = pl.BlockSpec((hist, pool_w), lambda i: (jnp.maximum(i * per_tile - 1, 0), 0))
    return pl.pallas_call(
        functools.partial(_merge_ffn_kernel, tiles_per_seq=seq // rows),
        grid=(m // rows,),
        in_specs=[row_spec(pool_w), prev_spec, _const_spec(u_meta.shape),
                  row_spec(d_model), row_spec(o.shape[1]), row_spec(d_model), row_spec(d_model)]
                 + [_const_spec(w.shape) for w in weights],
        out_specs=row_spec(d_model),
        out_shape=jax.ShapeDtypeStruct((m, d_model), F32),
        scratch_shapes=[pltpu.VMEM((hist + rows, pool_w), F32),
                        pltpu.VMEM((rows, d_model), BF16),
                        pltpu.VMEM((rows, d_model), F32)],
        compiler_params=pltpu.CompilerParams(
            dimension_semantics=("arbitrary",), vmem_limit_bytes=VMEM_LIMIT),
        name="merge_ffn",
    )(u, u, u_meta, x2d, o, gp, ga, *weights)


def _t5_bucket(rel):
    nb = N_BUCKETS // 2
    ret = jnp.where(rel > 0, nb, 0)
    n = jnp.abs(rel)
    max_exact = nb // 2
    nf = jnp.maximum(n, max_exact).astype(jnp.float32)
    large = max_exact + (jnp.log(nf / max_exact) / math.log(MAX_DISTANCE / max_exact)
                         * (nb - max_exact)).astype(jnp.int32)
    large = jnp.minimum(large, nb - 1)
    return ret + jnp.where(n < max_exact, n, large)


def _bias_kernel(w_ref, diag_ref, meta_ref):
    _, tk, tq = diag_ref.shape
    n_meta = meta_ref.shape[1]
    period = w_ref.shape[1]

    def toeplitz(row, n_rows):
        x = jnp.broadcast_to(w_ref[row:row + 1, :], (n_rows, period))
        return pltpu.roll(x, 0, 1, stride=1, stride_axis=0)[:, :tq]

    key = lax.broadcasted_iota(jnp.int32, (tk, tq), 0)
    qry = lax.broadcasted_iota(jnp.int32, (tk, tq), 1)
    diag_ref[0] = jnp.where(key // CHUNK <= qry // CHUNK, toeplitz(0, tk), NEG_INF)
    diag_ref[1] = toeplitz(1, tk)
    meta_ref[0] = toeplitz(2, n_meta)
    meta_ref[1] = jnp.broadcast_to(w_ref[3:4, :tq], (n_meta, tq))


def _bias_tables(rel_bias_table):
    tq, tk = ATTN_Q, ATTN_K
    n_heads = rel_bias_table.shape[1]
    assert tq == tk and tk % CHUNK == 0 and tk + 1 >= MAX_DISTANCE
    period = 2 * tk
    signed = np.where(np.arange(period) < tk, np.arange(period), np.arange(period) - period)
    rel = np.stack([
        -signed,
        -signed - tk,
        -signed - N_META,
        np.full(period, -(tk + N_META)),
    ])
    vectors = rel_bias_table[_t5_bucket(jnp.asarray(rel))].astype(F32) * LOG2_E
    vectors = jnp.transpose(vectors, (2, 0, 1))
    diag, meta = pl.pallas_call(
        _bias_kernel,
        grid=(n_heads,),
        in_specs=[pl.BlockSpec((None, 4, period), lambda h: (h, 0, 0))],
        out_specs=(pl.BlockSpec((None, 2, tk, tq), lambda h: (h, 0, 0, 0)),
                   pl.BlockSpec((None, 2, N_META, tq), lambda h: (h, 0, 0, 0))),
        out_shape=(jax.ShapeDtypeStruct((n_heads, 2, tk, tq), F32),
                   jax.ShapeDtypeStruct((n_heads, 2, N_META, tq), F32)),
        compiler_params=pltpu.CompilerParams(dimension_semantics=("arbitrary",)),
        name="bias_tiles",
    )(vectors)
    return diag, meta, vectors[:, 3, 0]


def kernel(x, meta_tokens, rel_bias_table, mix_norm_w, w_in, pool_group_w, pool_scale,
           lambda_q1, lambda_k1, lambda_q2, lambda_k2, subln_w, w_pool_out, w_attn_out,
           w_o, ffn_norm_w, w_gate, w_up, w_down, final_norm_w):
    batch, seq, d_model = x.shape
    assert w_in.shape[0] == 1, "single-layer block"
    pool_w = pool_scale.shape[1]
    attn_w = N_HEADS * V_DIM
    assert seq % ATTN_Q == 0 and seq % MERGE_ROWS == 0 and (batch * seq) % PROJ_ROWS == 0
    assert N_META >= max(POOL_WINDOWS) and MERGE_ROWS % N_META == 0

    w_in_bf = w_in[0].astype(BF16)
    x2d = x.reshape(batch * seq, d_model)
    u, q, k, v, gp, ga = _in_proj(x2d, mix_norm_w[0], w_in_bf, rows=PROJ_ROWS,
                                  pool_w=pool_w, attn_w=attn_w)
    u_meta, _, k_meta, v_meta, _, _ = _in_proj(meta_tokens.astype(x.dtype), mix_norm_w[0], w_in_bf,
                                               rows=N_META, pool_w=pool_w, attn_w=attn_w)
    v_meta_t = v_meta.T.reshape(N_HEADS, V_DIM, N_META)
    ones_rows = jnp.zeros((N_HEADS, BF16_ROWS, N_META), BF16).at[:, 0, :].set(1)
    v_meta_t = jnp.pad(jnp.concatenate([v_meta_t, ones_rows], axis=1),
                       ((0, 0), (0, 0), (0, LANES - N_META))).reshape(N_HEADS * V_ROWS, LANES)

    bias_diag, bias_meta, far_bias = _bias_tables(rel_bias_table)
    lam_init = 0.8 - 0.6 * math.exp(-0.3 * LAYER_IDX)
    lam_vecs = jnp.stack([lambda_q1[0], lambda_k1[0], lambda_q2[0], lambda_k2[0]]).astype(F32)
    o = _diff_attn(q, k, v, k_meta, v_meta_t, bias_diag, bias_meta, far_bias, lam_vecs,
                   subln_w[0].reshape(V_DIM, 1).astype(F32),
                   batch=batch, seq=seq, lam_init=lam_init)

    assert w_gate.shape[2] % FFN_COLS == 0
    weights = (
        pool_group_w[0].astype(BF16), pool_scale[0].reshape(1, pool_w),
        w_pool_out[0].astype(BF16), w_attn_out[0].astype(BF16), w_o[0].astype(BF16),
        ffn_norm_w[0].reshape(1, d_model),
        w_gate[0].astype(BF16), w_up[0].astype(BF16), w_down[0].astype(BF16),
        final_norm_w.reshape(1, d_model))
    out = _merge_ffn(u, u_meta, x2d, o, gp, ga, weights, seq=seq)
    return out.reshape(batch, seq, d_model)
```

```python
import functools
import itertools
import math

import jax
import jax.numpy as jnp
import numpy as np
from jax import lax
from jax.experimental import pallas as pl
from jax.experimental.pallas import tpu as pltpu

CHUNK = 64
N_META = 16
POOL_WINDOWS = (2, 4, 8, 16)
N_HEADS = 8
HEAD_DIM = 64
N_BUCKETS = 32
MAX_DISTANCE = 128
NORM_EPS = 1e-6
NEG_INF = -1e30
LOG2_E = math.log2(math.e)
LAYER_IDX = 0

V_DIM = 2 * HEAD_DIM
LANES = 128
BF16_ROWS = 16
V_ROWS = V_DIM + BF16_ROWS

PROJ_ROWS = 512
ATTN_Q = 512
ATTN_K = 512
STATE_COPIES = 3
ATTN_SEQS = 2
STAGES_PER_REGION = 13
SCORE_SLOTS = 2
COL_CHUNK = 256
MERGE_ROWS = 512
FFN_COLS = 256
VMEM_LIMIT = 56 * 1024 * 1024

BF16 = jnp.bfloat16
F32 = jnp.float32


def _const_spec(shape):
    return pl.BlockSpec(shape, lambda *_: (0,) * len(shape), pipeline_mode=pl.Buffered(1))


def _rms(x, w):
    var = jnp.mean(x * x, axis=-1, keepdims=True)
    return x * lax.rsqrt(var + NORM_EPS) * w


def _in_proj_kernel(x_ref, nw_ref, w_ref, u_ref, q_ref, k_ref, vt_ref, gp_ref, ga_ref,
                    *, pool_w, attn_w, d_model, transpose_v):
    xn = _rms(x_ref[...], nw_ref[...]).astype(BF16)

    def proj(lo, width):
        return jnp.dot(xn, w_ref[:, lo:lo + width], preferred_element_type=F32)

    lo = 0
    u_ref[...] = proj(lo, pool_w)
    lo += pool_w
    q_ref[...] = (proj(lo, attn_w) * (HEAD_DIM ** -0.5)).astype(BF16)
    lo += attn_w
    k_ref[...] = proj(lo, attn_w).astype(BF16)
    lo += attn_w
    v = proj(lo, attn_w)
    vt_ref[...] = (v.T if transpose_v else v).astype(BF16)
    lo += attn_w
    gp_ref[...] = jax.nn.sigmoid(proj(lo, d_model))
    lo += d_model
    ga_ref[...] = jax.nn.sigmoid(proj(lo, d_model))


def _in_proj(x2d, norm_w, w_in_bf, *, rows, pool_w, attn_w):
    m, d_model = x2d.shape
    in_cols = w_in_bf.shape[1]
    row_spec = lambda width: pl.BlockSpec((rows, width), lambda i: (i, 0))
    transpose_v = rows % LANES == 0
    out_shape = (
        jax.ShapeDtypeStruct((m, pool_w), F32),
        jax.ShapeDtypeStruct((m, attn_w), BF16),
        jax.ShapeDtypeStruct((m, attn_w), BF16),
        jax.ShapeDtypeStruct((attn_w, m) if transpose_v else (m, attn_w), BF16),
        jax.ShapeDtypeStruct((m, d_model), F32),
        jax.ShapeDtypeStruct((m, d_model), F32),
    )
    out_specs = [row_spec(s.shape[1]) for s in out_shape]
    if transpose_v:
        out_specs[3] = pl.BlockSpec((attn_w, rows), lambda i: (0, i))
    return pl.pallas_call(
        functools.partial(_in_proj_kernel, pool_w=pool_w, attn_w=attn_w, d_model=d_model,
                          transpose_v=transpose_v),
        grid=(m // rows,),
        in_specs=[row_spec(d_model), _const_spec((1, d_model)), _const_spec((d_model, in_cols))],
        out_specs=tuple(out_specs),
        out_shape=out_shape,
        compiler_params=pltpu.CompilerParams(
            dimension_semantics=("arbitrary",), vmem_limit_bytes=VMEM_LIMIT),
        name="in_proj",
    )(x2d, norm_w.reshape(1, d_model), w_in_bf)


def _attn_kernel(far_ref, q_ref, k_ref, vt_ref, km_ref, vmt_ref, bd_ref, bm_ref, lam_ref, sw_ref,
                 o_ref, vt_sc, qs_sc, s_sc, mx_sc, m_sc, acc_sc, *, lam_init, n_q):
    h = pl.program_id(1)
    n_blk, v_rows, tk = vt_sc.shape
    tq = tk
    far = far_ref[h]

    def stage_values():
        ones_row = lax.broadcasted_iota(jnp.int32, (v_rows - V_DIM, tk), 0) == 0
        for blk in range(n_blk):
            vt_sc[blk, :V_DIM, :] = vt_ref[:, blk * tk:(blk + 1) * tk]
            vt_sc[blk, V_DIM:, :] = ones_row.astype(BF16)

    def prologue(t):
        qt = q_ref[t * tq:(t + 1) * tq, :].astype(F32).T * LOG2_E
        row = lax.broadcasted_iota(jnp.int32, qt.shape, 0)
        zero = jnp.zeros_like(qt)
        qt = jnp.concatenate([jnp.where(row < HEAD_DIM, qt, zero),
                              jnp.where(row >= HEAD_DIM, qt, zero)], axis=1)
        hi = qt.astype(BF16)
        lo = (qt - hi.astype(F32)).astype(BF16)
        qs = jnp.concatenate([hi, lo], axis=0)
        qs_sc[t % STATE_COPIES] = qs
        bm = bm_ref[min(t % n_q, 1)]
        km = km_ref[...]
        s = jnp.dot(jnp.concatenate([km, km], axis=1), qs, preferred_element_type=F32)
        s = s + jnp.concatenate([bm, bm], axis=1)
        m0 = jnp.max(s, axis=0, keepdims=True)
        p = jnp.exp2(s - m0)
        m_sc[t % STATE_COPIES] = m0
        p_pad = jnp.concatenate(
            [p.astype(BF16), jnp.zeros((vmt_ref.shape[1] - p.shape[0], p.shape[1]), BF16)], axis=0)
        acc_sc[t % STATE_COPIES] = jnp.dot(vmt_ref[...], p_pad, preferred_element_type=F32)

    def live_keys(t, blk, c):
        return min(tk, c % tq + COL_CHUNK) if t == blk else tk

    def unit_a(slot, t, blk, c):
        cols = slice(c, c + COL_CHUNK)
        n_keys = live_keys(t, blk, c)
        k_blk = k_ref[blk * tk:blk * tk + n_keys, :]
        s = jnp.dot(jnp.concatenate([k_blk, k_blk], axis=1), qs_sc[t % STATE_COPIES, :, cols],
                    preferred_element_type=F32)
        if t - blk >= 2:
            shift = far
        else:
            s = s + bd_ref[t - blk, :n_keys, c % tq:c % tq + COL_CHUNK]
            shift = 0.0
        s_sc[slot, :n_keys, cols] = s
        mx_sc[slot, :, cols] = jnp.max(s, axis=0, keepdims=True) + shift

    def unit_b(slot, t, blk, c):
        shift = far if t - blk >= 2 else 0.0
        cols = slice(c, c + COL_CHUNK)
        n_keys = live_keys(t, blk, c)
        m_prev = m_sc[t % STATE_COPIES, :, cols]
        m_new = jnp.maximum(m_prev, mx_sc[slot, :, cols])
        alpha = jnp.exp2(m_prev - m_new)
        p = jnp.exp2(s_sc[slot, :n_keys, cols] - (m_new - shift))
        acc_sc[t % STATE_COPIES, :, cols] = alpha * acc_sc[t % STATE_COPIES, :, cols] + jnp.dot(
            vt_sc[blk, :, :n_keys], p.astype(BF16), preferred_element_type=F32)
        m_sc[t % STATE_COPIES, :, cols] = m_new

    def epilogue(t):
        lam = (jnp.exp(jnp.sum(lam_ref[0:1, :] * lam_ref[1:2, :], axis=1, keepdims=True))
               - jnp.exp(jnp.sum(lam_ref[2:3, :] * lam_ref[3:4, :], axis=1, keepdims=True))
               + lam_init)
        o = acc_sc[t % STATE_COPIES, :V_DIM, :] / acc_sc[t % STATE_COPIES, V_DIM:V_DIM + 1, :]
        o = o[:, :tq] - lam * o[:, tq:]
        var = jnp.mean(o * o, axis=0, keepdims=True)
        o = o * lax.rsqrt(var + NORM_EPS) * sw_ref[...] * (1.0 - lam_init)
        o_ref[t * tq:(t + 1) * tq, :] = o.T.astype(o_ref.dtype)

    units = [(t, blk) for t in range(n_blk) for blk in range(t - t % n_q, t + 1)]
    stages = [{"epilogue": [], "prologue": [], "a": [], "b": []} for _ in range(len(units) + 2)]
    for k, (t, blk) in enumerate(units):
        for c in range(0, 2 * tq, COL_CHUNK):
            stages[k]["a"].append(functools.partial(unit_a, k % SCORE_SLOTS, t, blk, c))
            stages[k + 1]["b"].append(functools.partial(unit_b, k % SCORE_SLOTS, t, blk, c))
        if blk % n_q == 0:
            stages[max(k - 1, 0)]["prologue"].append(functools.partial(prologue, t))
        if blk == t:
            stages[k + 2]["epilogue"].append(functools.partial(epilogue, t))
    stages[0]["epilogue"].append(stage_values)

    for first in range(0, len(stages), STAGES_PER_REGION):
        @pl.when(pl.program_id(0) >= -first)
        def _(group=stages[first:first + STAGES_PER_REGION]):
            for stage in group:
                for part in ("epilogue", "prologue"):
                    for fn in stage[part]:
                        fn()
                for pair in itertools.zip_longest(stage["a"], stage["b"]):
                    for fn in pair:
                        if fn is not None:
                            fn()


def _diff_attn(q, k, v_t, k_meta, v_meta_t, bias_diag, bias_meta, far_bias, lam_vecs, subln_w,
               *, batch, seq, lam_init):
    n_meta = k_meta.shape[0]
    rows = ATTN_SEQS * seq
    assert batch % ATTN_SEQS == 0
    seq_spec = pl.BlockSpec((rows, V_DIM), lambda b, h: (b, h))
    return pl.pallas_call(
        functools.partial(_attn_kernel, lam_init=lam_init, n_q=seq // ATTN_Q),
        grid=(batch // ATTN_SEQS, N_HEADS),
        in_specs=[
            pl.BlockSpec(memory_space=pltpu.SMEM),
            seq_spec, seq_spec,
            pl.BlockSpec((V_DIM, rows), lambda b, h: (h, b)),
            pl.BlockSpec((n_meta, V_DIM), lambda b, h: (0, h)),
            pl.BlockSpec((V_ROWS, LANES), lambda b, h: (h, 0)),
            pl.BlockSpec((None, 2, ATTN_K, ATTN_Q), lambda b, h: (h, 0, 0, 0)),
            pl.BlockSpec((None, 2, n_meta, ATTN_Q), lambda b, h: (h, 0, 0, 0)),
            pl.BlockSpec((4, HEAD_DIM), lambda b, h: (0, 0)),
            pl.BlockSpec((V_DIM, 1), lambda b, h: (0, 0)),
        ],
        out_specs=seq_spec,
        out_shape=jax.ShapeDtypeStruct(q.shape, BF16),
        scratch_shapes=[
            pltpu.VMEM((rows // ATTN_K, V_ROWS, ATTN_K), BF16),
            pltpu.VMEM((STATE_COPIES, 2 * V_DIM, 2 * ATTN_Q), BF16),
            pltpu.VMEM((SCORE_SLOTS, ATTN_K, 2 * ATTN_Q), F32),
            pltpu.VMEM((SCORE_SLOTS, 1, 2 * ATTN_Q), F32),
            pltpu.VMEM((STATE_COPIES, 1, 2 * ATTN_Q), F32),
            pltpu.VMEM((STATE_COPIES, V_ROWS, 2 * ATTN_Q), F32),
        ],
        compiler_params=pltpu.CompilerParams(
            dimension_semantics=("arbitrary", "arbitrary"),
            vmem_limit_bytes=VMEM_LIMIT),
        name="diff_attn",
    )(far_bias, q, k, v_t, k_meta, v_meta_t, bias_diag, bias_meta, lam_vecs, subln_w)


def _merge_ffn_kernel(u_ref, uprev_ref, umeta_ref, x_ref, o_ref, gp_ref, ga_ref,
                      gw_ref, ps_ref, wpo_ref, wao_ref, wo_ref, fnw_ref,
                      wg_ref, wu_ref, wd_ref, onw_ref, out_ref, ext_sc, hn_sc, acc_sc,
                      *, tiles_per_seq):
    rows = u_ref.shape[0]
    gdim = gw_ref.shape[1]
    hist = uprev_ref.shape[0]

    first = pl.program_id(0) % tiles_per_seq == 0
    ext_sc[0:hist, :] = jnp.where(first, umeta_ref[...], uprev_ref[...])
    ext_sc[hist:, :] = u_ref[...]

    pool_parts = []
    for g, win in enumerate(POOL_WINDOWS):
        cols = slice(g * gdim, (g + 1) * gdim)
        assert win & (win - 1) == 0 and win <= hist
        ext = ext_sc[:, cols]
        total = ext
        span = 1
        while span < win:
            total = total + pltpu.roll(total, span, 0)
            span *= 2
        pooled = total[hist:] / float(win) - ext[hist:]
        pool_parts.append(jnp.dot(pooled.astype(BF16), gw_ref[g], preferred_element_type=F32))
    pool_out = jnp.concatenate(pool_parts, axis=1) * ps_ref[...]

    merged = (gp_ref[...] * jnp.dot(pool_out.astype(BF16), wpo_ref[...],
                                    preferred_element_type=F32)
              + ga_ref[...] * jnp.dot(o_ref[...], wao_ref[...], preferred_element_type=F32))
    h1 = x_ref[...] + jnp.dot(merged.astype(BF16), wo_ref[...], preferred_element_type=F32)

    hn_sc[...] = _rms(h1, fnw_ref[...]).astype(BF16)
    acc_sc[...] = h1

    for c in range(0, wg_ref.shape[1], FFN_COLS):
        hn = hn_sc[...]
        gate = jnp.dot(hn, wg_ref[:, c:c + FFN_COLS], preferred_element_type=F32)
        up = jnp.dot(hn, wu_ref[:, c:c + FFN_COLS], preferred_element_type=F32)
        act = (jax.nn.silu(gate) * up).astype(BF16)
        acc_sc[...] += jnp.dot(act, wd_ref[c:c + FFN_COLS, :], preferred_element_type=F32)
    out_ref[...] = _rms(acc_sc[...], onw_ref[...])


def _merge_ffn(u, u_meta, x2d, o, gp, ga, weights, *, seq):
    m, d_model = x2d.shape
    pool_w = u.shape[1]
    hist = u_meta.shape[0]
    rows = MERGE_ROWS
    per_tile = rows // hist
    row_spec = lambda width: pl.BlockSpec((rows, width), lambda i: (i, 0))
    prev_spec = pl.BlockSpec((hist, pool_w), lambda i: (jnp.maximum(i * per_tile - 1, 0), 0))
    return pl.pallas_call(
        functools.partial(_merge_ffn_kernel, tiles_per_seq=seq // rows),
        grid=(m // rows,),
        in_specs=[row_spec(pool_w), prev_spec, _const_spec(u_meta.shape),
                  row_spec(d_model), row_spec(o.shape[1]), row_spec(d_model), row_spec(d_model)]
                 + [_const_spec(w.shape) for w in weights],
        out_specs=row_spec(d_model),
        out_shape=jax.ShapeDtypeStruct((m, d_model), F32),
        scratch_shapes=[pltpu.VMEM((hist + rows, pool_w), F32),
                        pltpu.VMEM((rows, d_model), BF16),
                        pltpu.VMEM((rows, d_model), F32)],
        compiler_params=pltpu.CompilerParams(
            dimension_semantics=("arbitrary",), vmem_limit_bytes=VMEM_LIMIT),
        name="merge_ffn",
    )(u, u, u_meta, x2d, o, gp, ga, *weights)


def _t5_bucket(rel):
    nb = N_BUCKETS // 2
    ret = jnp.where(rel > 0, nb, 0)
    n = jnp.abs(rel)
    max_exact = nb // 2
    nf = jnp.maximum(n, max_exact).astype(jnp.float32)
    large = max_exact + (jnp.log(nf / max_exact) / math.log(MAX_DISTANCE / max_exact)
                         * (nb - max_exact)).astype(jnp.int32)
    large = jnp.minimum(large, nb - 1)
    return ret + jnp.where(n < max_exact, n, large)


def _bias_kernel(w_ref, diag_ref, meta_ref):
    _, tk, tq = diag_ref.shape
    n_meta = meta_ref.shape[1]
    period = w_ref.shape[1]

    def toeplitz(row, n_rows):
        x = jnp.broadcast_to(w_ref[row:row + 1, :], (n_rows, period))
        return pltpu.roll(x, 0, 1, stride=1, stride_axis=0)[:, :tq]

    key = lax.broadcasted_iota(jnp.int32, (tk, tq), 0)
    qry = lax.broadcasted_iota(jnp.int32, (tk, tq), 1)
    diag_ref[0] = jnp.where(key // CHUNK <= qry // CHUNK, toeplitz(0, tk), NEG_INF)
    diag_ref[1] = toeplitz(1, tk)
    meta_ref[0] = toeplitz(2, n_meta)
    meta_ref[1] = jnp.broadcast_to(w_ref[3:4, :tq], (n_meta, tq))


def _bias_tables(rel_bias_table):
    tq, tk = ATTN_Q, ATTN_K
    n_heads = rel_bias_table.shape[1]
    assert tq == tk and tk % CHUNK == 0 and tk + 1 >= MAX_DISTANCE
    period = 2 * tk
    signed = np.where(np.arange(period) < tk, np.arange(period), np.arange(period) - period)
    rel = np.stack([
        -signed,
        -signed - tk,
        -signed - N_META,
        np.full(period, -(tk + N_META)),
    ])
    vectors = rel_bias_table[_t5_bucket(jnp.asarray(rel))].astype(F32) * LOG2_E
    vectors = jnp.transpose(vectors, (2, 0, 1))
    diag, meta = pl.pallas_call(
        _bias_kernel,
        grid=(n_heads,),
        in_specs=[pl.BlockSpec((None, 4, period), lambda h: (h, 0, 0))],
        out_specs=(pl.BlockSpec((None, 2, tk, tq), lambda h: (h, 0, 0, 0)),
                   pl.BlockSpec((None, 2, N_META, tq), lambda h: (h, 0, 0, 0))),
        out_shape=(jax.ShapeDtypeStruct((n_heads, 2, tk, tq), F32),
                   jax.ShapeDtypeStruct((n_heads, 2, N_META, tq), F32)),
        compiler_params=pltpu.CompilerParams(dimension_semantics=("arbitrary",)),
        name="bias_tiles",
    )(vectors)
    return diag, meta, vectors[:, 3, 0]


def kernel(x, meta_tokens, rel_bias_table, mix_norm_w, w_in, pool_group_w, pool_scale,
           lambda_q1, lambda_k1, lambda_q2, lambda_k2, subln_w, w_pool_out, w_attn_out,
           w_o, ffn_norm_w, w_gate, w_up, w_down, final_norm_w):
    batch, seq, d_model = x.shape
    assert w_in.shape[0] == 1, "single-layer block"
    pool_w = pool_scale.shape[1]
    attn_w = N_HEADS * V_DIM
    assert seq % ATTN_Q == 0 and seq % MERGE_ROWS == 0 and (batch * seq) % PROJ_ROWS == 0
    assert N_META >= max(POOL_WINDOWS) and MERGE_ROWS % N_META == 0

    w_in_bf = w_in[0].astype(BF16)
    x2d = x.reshape(batch * seq, d_model)
    u, q, k, v_t, gp, ga = _in_proj(x2d, mix_norm_w[0], w_in_bf, rows=PROJ_ROWS,
                                  pool_w=pool_w, attn_w=attn_w)
    u_meta, _, k_meta, v_meta, _, _ = _in_proj(meta_tokens.astype(x.dtype), mix_norm_w[0], w_in_bf,
                                               rows=N_META, pool_w=pool_w, attn_w=attn_w)
    v_meta_t = v_meta.T.reshape(N_HEADS, V_DIM, N_META)
    ones_rows = jnp.zeros((N_HEADS, BF16_ROWS, N_META), BF16).at[:, 0, :].set(1)
    v_meta_t = jnp.pad(jnp.concatenate([v_meta_t, ones_rows], axis=1),
                       ((0, 0), (0, 0), (0, LANES - N_META))).reshape(N_HEADS * V_ROWS, LANES)

    bias_diag, bias_meta, far_bias = _bias_tables(rel_bias_table)
    lam_init = 0.8 - 0.6 * math.exp(-0.3 * LAYER_IDX)
    lam_vecs = jnp.stack([lambda_q1[0], lambda_k1[0], lambda_q2[0], lambda_k2[0]]).astype(F32)
    o = _diff_attn(q, k, v_t, k_meta, v_meta_t, bias_diag, bias_meta, far_bias, lam_vecs,
                   subln_w[0].reshape(V_DIM, 1).astype(F32),
                   batch=batch, seq=seq, lam_init=lam_init)

    assert w_gate.shape[2] % FFN_COLS == 0
    weights = (
        pool_group_w[0].astype(BF16), pool_scale[0].reshape(1, pool_w),
        w_pool_out[0].astype(BF16), w_attn_out[0].astype(BF16), w_o[0].astype(BF16),
        ffn_norm_w[0].reshape(1, d_model),
        w_gate[0].astype(BF16), w_up[0].astype(BF16), w_down[0].astype(BF16),
        final_norm_w.reshape(1, d_model))
    out = _merge_ffn(u, u_meta, x2d, o, gp, ga, weights, seq=seq)
    return out.reshape(batch, seq, d_model)
```

```python
import functools
import itertools
import math

import jax
import jax.numpy as jnp
import numpy as np
from jax import lax
from jax.experimental import pallas as pl
from jax.experimental.pallas import tpu as pltpu

CHUNK = 64
N_META = 16
POOL_WINDOWS = (2, 4, 8, 16)
N_HEADS = 8
HEAD_DIM = 64
N_BUCKETS = 32
MAX_DISTANCE = 128
NORM_EPS = 1e-6
NEG_INF = -1e30
LOG2_E = math.log2(math.e)
LAYER_IDX = 0

V_DIM = 2 * HEAD_DIM
LANES = 128
BF16_ROWS = 16
V_ROWS = V_DIM + BF16_ROWS

PROJ_ROWS = 512
ATTN_Q = 512
ATTN_K = 512
STAGES_PER_REGION = 13
SCORE_SLOTS = 2
COL_CHUNK = 256
MERGE_ROWS = 512
FFN_COLS = 256
VMEM_LIMIT = 56 * 1024 * 1024

BF16 = jnp.bfloat16
F32 = jnp.float32


def _const_spec(shape):
    return pl.BlockSpec(shape, lambda *_: (0,) * len(shape), pipeline_mode=pl.Buffered(1))


def _rms(x, w):
    var = jnp.mean(x * x, axis=-1, keepdims=True)
    return x * lax.rsqrt(var + NORM_EPS) * w


def _in_proj_kernel(x_ref, nw_ref, w_ref, u_ref, q_ref, k_ref, v_ref, gp_ref, ga_ref,
                    *, pool_w, attn_w, d_model):
    xn = _rms(x_ref[...], nw_ref[...]).astype(BF16)

    def proj(lo, width):
        return jnp.dot(xn, w_ref[:, lo:lo + width], preferred_element_type=F32)

    lo_q = pool_w
    lo_k = lo_q + attn_w
    lo_v = lo_k + attn_w
    lo_gp = lo_v + attn_w
    lo_ga = lo_gp + d_model
    gp_ref[...] = jax.nn.sigmoid(proj(lo_gp, d_model))
    ga_ref[...] = jax.nn.sigmoid(proj(lo_ga, d_model))
    q_ref[...] = (proj(lo_q, attn_w) * (HEAD_DIM ** -0.5)).astype(BF16)
    k_ref[...] = proj(lo_k, attn_w).astype(BF16)
    v_ref[...] = proj(lo_v, attn_w).astype(BF16)
    u_ref[...] = proj(0, pool_w)


def _in_proj(x2d, norm_w, w_in_bf, *, rows, pool_w, attn_w):
    m, d_model = x2d.shape
    in_cols = w_in_bf.shape[1]
    row_spec = lambda width: pl.BlockSpec((rows, width), lambda i: (i, 0))
    out_shape = (
        jax.ShapeDtypeStruct((m, pool_w), F32),
        jax.ShapeDtypeStruct((m, attn_w), BF16),
        jax.ShapeDtypeStruct((m, attn_w), BF16),
        jax.ShapeDtypeStruct((m, attn_w), BF16),
        jax.ShapeDtypeStruct((m, d_model), F32),
        jax.ShapeDtypeStruct((m, d_model), F32),
    )
    return pl.pallas_call(
        functools.partial(_in_proj_kernel, pool_w=pool_w, attn_w=attn_w, d_model=d_model),
        grid=(m // rows,),
        in_specs=[row_spec(d_model), _const_spec((1, d_model)), _const_spec((d_model, in_cols))],
        out_specs=tuple(row_spec(s.shape[1]) for s in out_shape),
        out_shape=out_shape,
        compiler_params=pltpu.CompilerParams(
            dimension_semantics=("arbitrary",), vmem_limit_bytes=VMEM_LIMIT),
        name="in_proj",
    )(x2d, norm_w.reshape(1, d_model), w_in_bf)


def _attn_kernel(far_ref, q_ref, k_ref, v_ref, km_ref, vmt_ref, bd_ref, bm_ref, lam_ref, sw_ref,
                 o_ref, vt_sc, qs_sc, s_sc, mx_sc, m_sc, acc_sc, *, lam_init):
    h = pl.program_id(1)
    n_blk, v_rows, tk = vt_sc.shape
    tq = tk
    n_q = q_ref.shape[0] // tq
    far = far_ref[h]

    def transpose_values():
        ones_row = lax.broadcasted_iota(jnp.int32, (v_rows - V_DIM, tk), 0) == 0
        for blk in range(n_blk):
            vt_sc[blk, :V_DIM, :] = v_ref[blk * tk:(blk + 1) * tk, :].astype(F32).T.astype(BF16)
            vt_sc[blk, V_DIM:, :] = ones_row.astype(BF16)

    def prologue(t):
        qt = q_ref[t * tq:(t + 1) * tq, :].astype(F32).T * LOG2_E
        row = lax.broadcasted_iota(jnp.int32, qt.shape, 0)
        zero = jnp.zeros_like(qt)
        qt = jnp.concatenate([jnp.where(row < HEAD_DIM, qt, zero),
                              jnp.where(row >= HEAD_DIM, qt, zero)], axis=1)
        hi = qt.astype(BF16)
        lo = (qt - hi.astype(F32)).astype(BF16)
        qs = jnp.concatenate([hi, lo], axis=0)
        qs_sc[t % 2] = qs
        bm = bm_ref[min(t, 1)]
        km = km_ref[...]
        s = jnp.dot(jnp.concatenate([km, km], axis=1), qs, preferred_element_type=F32)
        s = s + jnp.concatenate([bm, bm], axis=1)
        m0 = jnp.max(s, axis=0, keepdims=True)
        p = jnp.exp2(s - m0)
        m_sc[t % 2] = m0
        p_pad = jnp.concatenate(
            [p.astype(BF16), jnp.zeros((vmt_ref.shape[1] - p.shape[0], p.shape[1]), BF16)], axis=0)
        acc_sc[t % 2] = jnp.dot(vmt_ref[...], p_pad, preferred_element_type=F32)

    def live_keys(t, blk, c):
        return min(tk, c % tq + COL_CHUNK) if t == blk else tk

    def unit_a(slot, t, blk, c):
        cols = slice(c, c + COL_CHUNK)
        n_keys = live_keys(t, blk, c)
        k_blk = k_ref[blk * tk:blk * tk + n_keys, :]
        s = jnp.dot(jnp.concatenate([k_blk, k_blk], axis=1), qs_sc[t % 2, :, cols],
                    preferred_element_type=F32)
        if t - blk >= 2:
            shift = far
        else:
            s = s + bd_ref[t - blk, :n_keys, c % tq:c % tq + COL_CHUNK]
            shift = 0.0
        s_sc[slot, :n_keys, cols] = s
        mx_sc[slot, :, cols] = jnp.max(s, axis=0, keepdims=True) + shift

    def unit_b(slot, t, blk, c):
        shift = far if t - blk >= 2 else 0.0
        cols = slice(c, c + COL_CHUNK)
        n_keys = live_keys(t, blk, c)
        m_prev = m_sc[t % 2, :, cols]
        m_new = jnp.maximum(m_prev, mx_sc[slot, :, cols])
        alpha = jnp.exp2(m_prev - m_new)
        p = jnp.exp2(s_sc[slot, :n_keys, cols] - (m_new - shift))
        acc_sc[t % 2, :, cols] = alpha * acc_sc[t % 2, :, cols] + jnp.dot(
            vt_sc[blk, :, :n_keys], p.astype(BF16), preferred_element_type=F32)
        m_sc[t % 2, :, cols] = m_new

    def epilogue(t):
        lam = (jnp.exp(jnp.sum(lam_ref[0:1, :] * lam_ref[1:2, :], axis=1, keepdims=True))
               - jnp.exp(jnp.sum(lam_ref[2:3, :] * lam_ref[3:4, :], axis=1, keepdims=True))
               + lam_init)
        o = acc_sc[t % 2, :V_DIM, :] / acc_sc[t % 2, V_DIM:V_DIM + 1, :]
        o = o[:, :tq] - lam * o[:, tq:]
        var = jnp.mean(o * o, axis=0, keepdims=True)
        o = o * lax.rsqrt(var + NORM_EPS) * sw_ref[...] * (1.0 - lam_init)
        o_ref[t * tq:(t + 1) * tq, :] = o.T.astype(o_ref.dtype)

    units = [(t, blk) for t in range(n_q) for blk in range(t + 1)]
    stages = [{"epilogue": [], "prologue": [], "a": [], "b": []} for _ in range(len(units) + 2)]
    for k, (t, blk) in enumerate(units):
        for c in range(0, 2 * tq, COL_CHUNK):
            stages[k]["a"].append(functools.partial(unit_a, k % SCORE_SLOTS, t, blk, c))
            stages[k + 1]["b"].append(functools.partial(unit_b, k % SCORE_SLOTS, t, blk, c))
        if blk == 0:
            stages[max(k - 1, 0)]["prologue"].append(functools.partial(prologue, t))
        if blk == t:
            stages[k + 2]["epilogue"].append(functools.partial(epilogue, t))
    stages[0]["epilogue"].append(transpose_values)

    for first in range(0, len(stages), STAGES_PER_REGION):
        @pl.when(pl.program_id(0) >= -first)
        def _(group=stages[first:first + STAGES_PER_REGION]):
            for stage in group:
                for part in ("epilogue", "prologue"):
                    for fn in stage[part]:
                        fn()
                for pair in itertools.zip_longest(stage["a"], stage["b"]):
                    for fn in pair:
                        if fn is not None:
                            fn()


def _diff_attn(q, k, v, k_meta, v_meta_t, bias_diag, bias_meta, far_bias, lam_vecs, subln_w,
               *, batch, seq, lam_init):
    n_meta = k_meta.shape[0]
    seq_spec = pl.BlockSpec((seq, V_DIM), lambda b, h: (b, h))
    return pl.pallas_call(
        functools.partial(_attn_kernel, lam_init=lam_init),
        grid=(batch, N_HEADS),
        in_specs=[
            pl.BlockSpec(memory_space=pltpu.SMEM),
            seq_spec, seq_spec, seq_spec,
            pl.BlockSpec((n_meta, V_DIM), lambda b, h: (0, h)),
            pl.BlockSpec((V_ROWS, LANES), lambda b, h: (h, 0)),
            pl.BlockSpec((None, 2, ATTN_K, ATTN_Q), lambda b, h: (h, 0, 0, 0)),
            pl.BlockSpec((None, 2, n_meta, ATTN_Q), lambda b, h: (h, 0, 0, 0)),
            pl.BlockSpec((4, HEAD_DIM), lambda b, h: (0, 0)),
            pl.BlockSpec((V_DIM, 1), lambda b, h: (0, 0)),
        ],
        out_specs=seq_spec,
        out_shape=jax.ShapeDtypeStruct(q.shape, BF16),
        scratch_shapes=[
            pltpu.VMEM((seq // ATTN_K, V_ROWS, ATTN_K), BF16),
            pltpu.VMEM((2, 2 * V_DIM, 2 * ATTN_Q), BF16),
            pltpu.VMEM((SCORE_SLOTS, ATTN_K, 2 * ATTN_Q), F32),
            pltpu.VMEM((SCORE_SLOTS, 1, 2 * ATTN_Q), F32),
            pltpu.VMEM((2, 1, 2 * ATTN_Q), F32),
            pltpu.VMEM((2, V_ROWS, 2 * ATTN_Q), F32),
        ],
        compiler_params=pltpu.CompilerParams(
            dimension_semantics=("arbitrary", "arbitrary"),
            vmem_limit_bytes=VMEM_LIMIT),
        name="diff_attn",
    )(far_bias, q, k, v, k_meta, v_meta_t, bias_diag, bias_meta, lam_vecs, subln_w)


def _merge_ffn_kernel(u_ref, uprev_ref, umeta_ref, x_ref, o_ref, gp_ref, ga_ref,
                      gw_ref, ps_ref, wpo_ref, wao_ref, wo_ref, fnw_ref,
                      wg_ref, wu_ref, wd_ref, onw_ref, out_ref, ext_sc, hn_sc, acc_sc,
                      *, tiles_per_seq):
    rows = u_ref.shape[0]
    gdim = gw_ref.shape[1]
    hist = uprev_ref.shape[0]

    first = pl.program_id(0) % tiles_per_seq == 0
    ext_sc[0:hist, :] = jnp.where(first, umeta_ref[...], uprev_ref[...])
    ext_sc[hist:, :] = u_ref[...]

    d_model = wao_ref.shape[1]
    piece = d_model // len(POOL_WINDOWS)
    attn_parts = []
    pool_parts = []
    for g, win in enumerate(POOL_WINDOWS):
        attn_parts.append(jnp.dot(o_ref[...], wao_ref[:, g * piece:(g + 1) * piece],
                                  preferred_element_type=F32))
        cols = slice(g * gdim, (g + 1) * gdim)
        assert win & (win - 1) == 0 and win <= hist
        ext = ext_sc[:, cols]
        total = ext
        span = 1
        while span < win:
            total = total + pltpu.roll(total, span, 0)
            span *= 2
        pooled = total[hist:] / float(win) - ext[hist:]
        pool_parts.append(jnp.dot(pooled.astype(BF16), gw_ref[g], preferred_element_type=F32))
    pool_out = jnp.concatenate(pool_parts, axis=1) * ps_ref[...]

    merged = (gp_ref[...] * jnp.dot(pool_out.astype(BF16), wpo_ref[...],
                                    preferred_element_type=F32)
              + ga_ref[...] * jnp.concatenate(attn_parts, axis=1))
    h1 = x_ref[...] + jnp.dot(merged.astype(BF16), wo_ref[...], preferred_element_type=F32)

    hn_sc[...] = _rms(h1, fnw_ref[...]).astype(BF16)
    acc_sc[...] = h1

    for c in range(0, wg_ref.shape[1], FFN_COLS):
        hn = hn_sc[...]
        gate = jnp.dot(hn, wg_ref[:, c:c + FFN_COLS], preferred_element_type=F32)
        up = jnp.dot(hn, wu_ref[:, c:c + FFN_COLS], preferred_element_type=F32)
        act = (jax.nn.silu(gate) * up).astype(BF16)
        acc_sc[...] += jnp.dot(act, wd_ref[c:c + FFN_COLS, :], preferred_element_type=F32)
    out_ref[...] = _rms(acc_sc[...], onw_ref[...])


def _merge_ffn(u, u_meta, x2d, o, gp, ga, weights, *, seq):
    m, d_model = x2d.shape
    pool_w = u.shape[1]
    hist = u_meta.shape[0]
    rows = MERGE_ROWS
    per_tile = rows // hist
    row_spec = lambda width: pl.BlockSpec((rows, width), lambda i: (i, 0))
    prev_spec = pl.BlockSpec((hist, pool_w), lambda i: (jnp.maximum(i * per_tile - 1, 0), 0))
    return pl.pallas_call(
        functools.partial(_merge_ffn_kernel, tiles_per_seq=seq // rows),
        grid=(m // rows,),
        in_specs=[row_spec(pool_w), prev_spec, _const_spec(u_meta.shape),
                  row_spec(d_model), row_spec(o.shape[1]), row_spec(d_model), row_spec(d_model)]
                 + [_const_spec(w.shape) for w in weights],
        out_specs=row_spec(d_model),
        out_shape=jax.ShapeDtypeStruct((m, d_model), F32),
        scratch_shapes=[pltpu.VMEM((hist + rows, pool_w), F32),
                        pltpu.VMEM((rows, d_model), BF16),
                        pltpu.VMEM((rows, d_model), F32)],
        compiler_params=pltpu.CompilerParams(
            dimension_semantics=("arbitrary",), vmem_limit_bytes=VMEM_LIMIT),
        name="merge_ffn",
    )(u, u, u_meta, x2d, o, gp, ga, *weights)


def _t5_bucket(rel):
    nb = N_BUCKETS // 2
    ret = jnp.where(rel > 0, nb, 0)
    n = jnp.abs(rel)
    max_exact = nb // 2
    nf = jnp.maximum(n, max_exact).astype(jnp.float32)
    large = max_exact + (jnp.log(nf / max_exact) / math.log(MAX_DISTANCE / max_exact)
                         * (nb - max_exact)).astype(jnp.int32)
    large = jnp.minimum(large, nb - 1)
    return ret + jnp.where(n < max_exact, n, large)


def _bias_kernel(w_ref, diag_ref, meta_ref):
    _, tk, tq = diag_ref.shape
    n_meta = meta_ref.shape[1]
    period = w_ref.shape[1]

    def toeplitz(row, n_rows):
        x = jnp.broadcast_to(w_ref[row:row + 1, :], (n_rows, period))
        return pltpu.roll(x, 0, 1, stride=1, stride_axis=0)[:, :tq]

    key = lax.broadcasted_iota(jnp.int32, (tk, tq), 0)
    qry = lax.broadcasted_iota(jnp.int32, (tk, tq), 1)
    diag_ref[0] = jnp.where(key // CHUNK <= qry // CHUNK, toeplitz(0, tk), NEG_INF)
    diag_ref[1] = toeplitz(1, tk)
    meta_ref[0] = toeplitz(2, n_meta)
    meta_ref[1] = jnp.broadcast_to(w_ref[3:4, :tq], (n_meta, tq))


def _bias_tables(rel_bias_table):
    tq, tk = ATTN_Q, ATTN_K
    n_heads = rel_bias_table.shape[1]
    assert tq == tk and tk % CHUNK == 0 and tk + 1 >= MAX_DISTANCE
    period = 2 * tk
    signed = np.where(np.arange(period) < tk, np.arange(period), np.arange(period) - period)
    rel = np.stack([
        -signed,
        -signed - tk,
        -signed - N_META,
        np.full(period, -(tk + N_META)),
    ])
    vectors = rel_bias_table[_t5_bucket(jnp.asarray(rel))].astype(F32) * LOG2_E
    vectors = jnp.transpose(vectors, (2, 0, 1))
    diag, meta = pl.pallas_call(
        _bias_kernel,
        grid=(n_heads,),
        in_specs=[pl.BlockSpec((None, 4, period), lambda h: (h, 0, 0))],
        out_specs=(pl.BlockSpec((None, 2, tk, tq), lambda h: (h, 0, 0, 0)),
                   pl.BlockSpec((None, 2, N_META, tq), lambda h: (h, 0, 0, 0))),
        out_shape=(jax.ShapeDtypeStruct((n_heads, 2, tk, tq), F32),
                   jax.ShapeDtypeStruct((n_heads, 2, N_META, tq), F32)),
        compiler_params=pltpu.CompilerParams(dimension_semantics=("arbitrary",)),
        name="bias_tiles",
    )(vectors)
    return diag, meta, vectors[:, 3, 0]


def kernel(x, meta_tokens, rel_bias_table, mix_norm_w, w_in, pool_group_w, pool_scale,
           lambda_q1, lambda_k1, lambda_q2, lambda_k2, subln_w, w_pool_out, w_attn_out,
           w_o, ffn_norm_w, w_gate, w_up, w_down, final_norm_w):
    batch, seq, d_model = x.shape
    assert w_in.shape[0] == 1, "single-layer block"
    pool_w = pool_scale.shape[1]
    attn_w = N_HEADS * V_DIM
    assert seq % ATTN_Q == 0 and seq % MERGE_ROWS == 0 and (batch * seq) % PROJ_ROWS == 0
    assert N_META >= max(POOL_WINDOWS) and MERGE_ROWS % N_META == 0

    w_in_bf = w_in[0].astype(BF16)
    x2d = x.reshape(batch * seq, d_model)
    u, q, k, v, gp, ga = _in_proj(x2d, mix_norm_w[0], w_in_bf, rows=PROJ_ROWS,
                                  pool_w=pool_w, attn_w=attn_w)
    u_meta, _, k_meta, v_meta, _, _ = _in_proj(meta_tokens.astype(x.dtype), mix_norm_w[0], w_in_bf,
                                               rows=N_META, pool_w=pool_w, attn_w=attn_w)
    v_meta_t = v_meta.T.reshape(N_HEADS, V_DIM, N_META)
    ones_rows = jnp.zeros((N_HEADS, BF16_ROWS, N_META), BF16).at[:, 0, :].set(1)
    v_meta_t = jnp.pad(jnp.concatenate([v_meta_t, ones_rows], axis=1),
                       ((0, 0), (0, 0), (0, LANES - N_META))).reshape(N_HEADS * V_ROWS, LANES)

    bias_diag, bias_meta, far_bias = _bias_tables(rel_bias_table)
    lam_init = 0.8 - 0.6 * math.exp(-0.3 * LAYER_IDX)
    lam_vecs = jnp.stack([lambda_q1[0], lambda_k1[0], lambda_q2[0], lambda_k2[0]]).astype(F32)
    o = _diff_attn(q, k, v, k_meta, v_meta_t, bias_diag, bias_meta, far_bias, lam_vecs,
                   subln_w[0].reshape(V_DIM, 1).astype(F32),
                   batch=batch, seq=seq, lam_init=lam_init)

    assert w_gate.shape[2] % FFN_COLS == 0
    weights = (
        pool_group_w[0].astype(BF16), pool_scale[0].reshape(1, pool_w),
        w_pool_out[0].astype(BF16), w_attn_out[0].astype(BF16), w_o[0].astype(BF16),
        ffn_norm_w[0].reshape(1, d_model),
        w_gate[0].astype(BF16), w_up[0].astype(BF16), w_down[0].astype(BF16),
        final_norm_w.reshape(1, d_model))
    out = _merge_ffn(u, u_meta, x2d, o, gp, ga, weights, seq=seq)
    return out.reshape(batch, seq, d_model)
```

```python
import functools
import itertools
import math

import jax
import jax.numpy as jnp
import numpy as np
from jax import lax
from jax.experimental import pallas as pl
from jax.experimental.pallas import tpu as pltpu

CHUNK = 64
N_META = 16
POOL_WINDOWS = (2, 4, 8, 16)
N_HEADS = 8
HEAD_DIM = 64
N_BUCKETS = 32
MAX_DISTANCE = 128
NORM_EPS = 1e-6
NEG_INF = -1e30
LOG2_E = math.log2(math.e)
LAYER_IDX = 0

V_DIM = 2 * HEAD_DIM
LANES = 128
BF16_ROWS = 16
V_ROWS = V_DIM + BF16_ROWS

PROJ_ROWS = 512
ATTN_Q = 512
ATTN_K = 512
STAGES_PER_REGION = 13
SCORE_SLOTS = 2
COL_CHUNK = 256
MERGE_ROWS = 512
FFN_COLS = 256
VMEM_LIMIT = 56 * 1024 * 1024

BF16 = jnp.bfloat16
F32 = jnp.float32


def _const_spec(shape):
    return pl.BlockSpec(shape, lambda *_: (0,) * len(shape), pipeline_mode=pl.Buffered(1))


def _rms(x, w):
    var = jnp.mean(x * x, axis=-1, keepdims=True)
    return x * lax.rsqrt(var + NORM_EPS) * w


def _in_proj_kernel(x_ref, nw_ref, w_ref, u_ref, q_ref, k_ref, v_ref, gp_ref, ga_ref,
                    *, pool_w, attn_w, d_model):
    xn = _rms(x_ref[...], nw_ref[...]).astype(BF16)

    def proj(lo, width):
        return jnp.dot(xn, w_ref[:, lo:lo + width], preferred_element_type=F32)

    lo_q = pool_w
    lo_k = lo_q + attn_w
    lo_v = lo_k + attn_w
    lo_gp = lo_v + attn_w
    lo_ga = lo_gp + d_model
    gp_ref[...] = jax.nn.sigmoid(proj(lo_gp, d_model))
    ga_ref[...] = jax.nn.sigmoid(proj(lo_ga, d_model))
    q_ref[...] = (proj(lo_q, attn_w) * (HEAD_DIM ** -0.5)).astype(BF16)
    k_ref[...] = proj(lo_k, attn_w).astype(BF16)
    v_ref[...] = proj(lo_v, attn_w).astype(BF16)
    u_ref[...] = proj(0, pool_w)


def _in_proj(x2d, norm_w, w_in_bf, *, rows, pool_w, attn_w):
    m, d_model = x2d.shape
    in_cols = w_in_bf.shape[1]
    row_spec = lambda width: pl.BlockSpec((rows, width), lambda i: (i, 0))
    out_shape = (
        jax.ShapeDtypeStruct((m, pool_w), F32),
        jax.ShapeDtypeStruct((m, attn_w), BF16),
        jax.ShapeDtypeStruct((m, attn_w), BF16),
        jax.ShapeDtypeStruct((m, attn_w), BF16),
        jax.ShapeDtypeStruct((m, d_model), F32),
        jax.ShapeDtypeStruct((m, d_model), F32),
    )
    return pl.pallas_call(
        functools.partial(_in_proj_kernel, pool_w=pool_w, attn_w=attn_w, d_model=d_model),
        grid=(m // rows,),
        in_specs=[row_spec(d_model), _const_spec((1, d_model)), _const_spec((d_model, in_cols))],
        out_specs=tuple(row_spec(s.shape[1]) for s in out_shape),
        out_shape=out_shape,
        compiler_params=pltpu.CompilerParams(
            dimension_semantics=("arbitrary",), vmem_limit_bytes=VMEM_LIMIT),
        name="in_proj",
    )(x2d, norm_w.reshape(1, d_model), w_in_bf)


def _attn_kernel(far_ref, q_ref, k_ref, v_ref, km_ref, vmt_ref, bd_ref, bm_ref, lam_ref, sw_ref,
                 o_ref, vt_sc, qs_sc, s_sc, mx_sc, m_sc, acc_sc, *, lam_init):
    h = pl.program_id(1)
    n_blk, v_rows, tk = vt_sc.shape
    tq = tk
    n_q = q_ref.shape[0] // tq
    far = far_ref[h]

    def transpose_values():
        ones_row = lax.broadcasted_iota(jnp.int32, (v_rows - V_DIM, tk), 0) == 0
        for blk in range(n_blk):
            vt_sc[blk, :V_DIM, :] = v_ref[blk * tk:(blk + 1) * tk, :].astype(F32).T.astype(BF16)
            vt_sc[blk, V_DIM:, :] = ones_row.astype(BF16)

    def prologue(t):
        qt = q_ref[t * tq:(t + 1) * tq, :].astype(F32).T * LOG2_E
        row = lax.broadcasted_iota(jnp.int32, qt.shape, 0)
        zero = jnp.zeros_like(qt)
        qt = jnp.concatenate([jnp.where(row < HEAD_DIM, qt, zero),
                              jnp.where(row >= HEAD_DIM, qt, zero)], axis=1)
        hi = qt.astype(BF16)
        lo = (qt - hi.astype(F32)).astype(BF16)
        qs = jnp.concatenate([hi, lo], axis=0)
        qs_sc[t % 2] = qs
        bm = bm_ref[min(t, 1)]
        km = km_ref[...]
        s = jnp.dot(jnp.concatenate([km, km], axis=1), qs, preferred_element_type=F32)
        s = s + jnp.concatenate([bm, bm], axis=1)
        m0 = jnp.max(s, axis=0, keepdims=True)
        p = jnp.exp2(s - m0)
        m_sc[t % 2] = m0
        p_pad = jnp.concatenate(
            [p.astype(BF16), jnp.zeros((vmt_ref.shape[1] - p.shape[0], p.shape[1]), BF16)], axis=0)
        acc_sc[t % 2] = jnp.dot(vmt_ref[...], p_pad, preferred_element_type=F32)

    def live_keys(t, blk, c):
        return min(tk, c % tq + COL_CHUNK) if t == blk else tk

    def unit_a(slot, t, blk, c):
        cols = slice(c, c + COL_CHUNK)
        n_keys = live_keys(t, blk, c)
        k_blk = k_ref[blk * tk:blk * tk + n_keys, :]
        s = jnp.dot(jnp.concatenate([k_blk, k_blk], axis=1), qs_sc[t % 2, :, cols],
                    preferred_element_type=F32)
        if t - blk >= 2:
            shift = far
        else:
            s = s + bd_ref[t - blk, :n_keys, c % tq:c % tq + COL_CHUNK]
            shift = 0.0
        s_sc[slot, :n_keys, cols] = s
        mx_sc[slot, :, cols] = jnp.max(s, axis=0, keepdims=True) + shift

    def unit_b(slot, t, blk, c):
        shift = far if t - blk >= 2 else 0.0
        cols = slice(c, c + COL_CHUNK)
        n_keys = live_keys(t, blk, c)
        m_prev = m_sc[t % 2, :, cols]
        m_new = jnp.maximum(m_prev, mx_sc[slot, :, cols])
        alpha = jnp.exp2(m_prev - m_new)
        p = jnp.exp2(s_sc[slot, :n_keys, cols] - (m_new - shift))
        acc_sc[t % 2, :, cols] = alpha * acc_sc[t % 2, :, cols] + jnp.dot(
            vt_sc[blk, :, :n_keys], p.astype(BF16), preferred_element_type=F32)
        m_sc[t % 2, :, cols] = m_new

    def epilogue(t):
        lam = (jnp.exp(jnp.sum(lam_ref[0:1, :] * lam_ref[1:2, :], axis=1, keepdims=True))
               - jnp.exp(jnp.sum(lam_ref[2:3, :] * lam_ref[3:4, :], axis=1, keepdims=True))
               + lam_init)
        o = acc_sc[t % 2, :V_DIM, :] / acc_sc[t % 2, V_DIM:V_DIM + 1, :]
        o = o[:, :tq] - lam * o[:, tq:]
        var = jnp.mean(o * o, axis=0, keepdims=True)
        o = o * lax.rsqrt(var + NORM_EPS) * sw_ref[...] * (1.0 - lam_init)
        o_ref[t * tq:(t + 1) * tq, :] = o.T.astype(o_ref.dtype)

    units = [(t, blk) for t in range(n_q) for blk in range(t + 1)]
    stages = [{"epilogue": [], "prologue": [], "a": [], "b": []} for _ in range(len(units) + 2)]
    chunk_order = [m * tq + c for c in range(0, tq, COL_CHUNK) for m in range(2)]
    for k, (t, blk) in enumerate(units):
        for c in chunk_order:
            stages[k]["a"].append(functools.partial(unit_a, k % SCORE_SLOTS, t, blk, c))
            stages[k + 1]["b"].append(functools.partial(unit_b, k % SCORE_SLOTS, t, blk, c))
        if blk == 0:
            stages[max(k - 1, 0)]["prologue"].append(functools.partial(prologue, t))
        if blk == t:
            stages[k + 2]["epilogue"].append(functools.partial(epilogue, t))
    stages[0]["epilogue"].append(transpose_values)

    for first in range(0, len(stages), STAGES_PER_REGION):
        @pl.when(pl.program_id(0) >= -first)
        def _(group=stages[first:first + STAGES_PER_REGION]):
            for stage in group:
                for part in ("epilogue", "prologue"):
                    for fn in stage[part]:
                        fn()
                for pair in itertools.zip_longest(stage["a"], stage["b"]):
                    for fn in pair:
                        if fn is not None:
                            fn()


def _diff_attn(q, k, v, k_meta, v_meta_t, bias_diag, bias_meta, far_bias, lam_vecs, subln_w,
               *, batch, seq, lam_init):
    n_meta = k_meta.shape[0]
    seq_spec = pl.BlockSpec((seq, V_DIM), lambda b, h: (b, h))
    return pl.pallas_call(
        functools.partial(_attn_kernel, lam_init=lam_init),
        grid=(batch, N_HEADS),
        in_specs=[
            pl.BlockSpec(memory_space=pltpu.SMEM),
            seq_spec, seq_spec, seq_spec,
            pl.BlockSpec((n_meta, V_DIM), lambda b, h: (0, h)),
            pl.BlockSpec((V_ROWS, LANES), lambda b, h: (h, 0)),
            pl.BlockSpec((None, 2, ATTN_K, ATTN_Q), lambda b, h: (h, 0, 0, 0)),
            pl.BlockSpec((None, 2, n_meta, ATTN_Q), lambda b, h: (h, 0, 0, 0)),
            pl.BlockSpec((4, HEAD_DIM), lambda b, h: (0, 0)),
            pl.BlockSpec((V_DIM, 1), lambda b, h: (0, 0)),
        ],
        out_specs=seq_spec,
        out_shape=jax.ShapeDtypeStruct(q.shape, BF16),
        scratch_shapes=[
            pltpu.VMEM((seq // ATTN_K, V_ROWS, ATTN_K), BF16),
            pltpu.VMEM((2, 2 * V_DIM, 2 * ATTN_Q), BF16),
            pltpu.VMEM((SCORE_SLOTS, ATTN_K, 2 * ATTN_Q), F32),
            pltpu.VMEM((SCORE_SLOTS, 1, 2 * ATTN_Q), F32),
            pltpu.VMEM((2, 1, 2 * ATTN_Q), F32),
            pltpu.VMEM((2, V_ROWS, 2 * ATTN_Q), F32),
        ],
        compiler_params=pltpu.CompilerParams(
            dimension_semantics=("arbitrary", "arbitrary"),
            vmem_limit_bytes=VMEM_LIMIT),
        name="diff_attn",
    )(far_bias, q, k, v, k_meta, v_meta_t, bias_diag, bias_meta, lam_vecs, subln_w)


def _merge_ffn_kernel(u_ref, uprev_ref, umeta_ref, x_ref, o_ref, gp_ref, ga_ref,
                      gw_ref, ps_ref, wpo_ref, wao_ref, wo_ref, fnw_ref,
                      wg_ref, wu_ref, wd_ref, onw_ref, out_ref, ext_sc, hn_sc, acc_sc,
                      *, tiles_per_seq):
    rows = u_ref.shape[0]
    gdim = gw_ref.shape[1]
    hist = uprev_ref.shape[0]

    first = pl.program_id(0) % tiles_per_seq == 0
    ext_sc[0:hist, :] = jnp.where(first, umeta_ref[...], uprev_ref[...])
    ext_sc[hist:, :] = u_ref[...]

    d_model = wao_ref.shape[1]
    piece = d_model // len(POOL_WINDOWS)
    attn_parts = []
    pool_parts = []
    for g, win in enumerate(POOL_WINDOWS):
        attn_parts.append(jnp.dot(o_ref[...], wao_ref[:, g * piece:(g + 1) * piece],
                                  preferred_element_type=F32))
        cols = slice(g * gdim, (g + 1) * gdim)
        assert win & (win - 1) == 0 and win <= hist
        ext = ext_sc[:, cols]
        total = ext
        span = 1
        while span < win:
            total = total + pltpu.roll(total, span, 0)
            span *= 2
        pooled = total[hist:] / float(win) - ext[hist:]
        pool_parts.append(jnp.dot(pooled.astype(BF16), gw_ref[g], preferred_element_type=F32))
    pool_out = jnp.concatenate(pool_parts, axis=1) * ps_ref[...]

    merged = (gp_ref[...] * jnp.dot(pool_out.astype(BF16), wpo_ref[...],
                                    preferred_element_type=F32)
              + ga_ref[...] * jnp.concatenate(attn_parts, axis=1))
    h1 = x_ref[...] + jnp.dot(merged.astype(BF16), wo_ref[...], preferred_element_type=F32)

    hn_sc[...] = _rms(h1, fnw_ref[...]).astype(BF16)
    acc_sc[...] = h1

    for c in range(0, wg_ref.shape[1], FFN_COLS):
        hn = hn_sc[...]
        gate = jnp.dot(hn, wg_ref[:, c:c + FFN_COLS], preferred_element_type=F32)
        up = jnp.dot(hn, wu_ref[:, c:c + FFN_COLS], preferred_element_type=F32)
        act = (jax.nn.silu(gate) * up).astype(BF16)
        acc_sc[...] += jnp.dot(act, wd_ref[c:c + FFN_COLS, :], preferred_element_type=F32)
    out_ref[...] = _rms(acc_sc[...], onw_ref[...])


def _merge_ffn(u, u_meta, x2d, o, gp, ga, weights, *, seq):
    m, d_model = x2d.shape
    pool_w = u.shape[1]
    hist = u_meta.shape[0]
    rows = MERGE_ROWS
    per_tile = rows // hist
    row_spec = lambda width: pl.BlockSpec((rows, width), lambda i: (i, 0))
    prev_spec = pl.BlockSpec((hist, pool_w), lambda i: (jnp.maximum(i * per_tile - 1, 0), 0))
    return pl.pallas_call(
        functools.partial(_merge_ffn_kernel, tiles_per_seq=seq // rows),
        grid=(m // rows,),
        in_specs=[row_spec(pool_w), prev_spec, _const_spec(u_meta.shape),
                  row_spec(d_model), row_spec(o.shape[1]), row_spec(d_model), row_spec(d_model)]
                 + [_const_spec(w.shape) for w in weights],
        out_specs=row_spec(d_model),
        out_shape=jax.ShapeDtypeStruct((m, d_model), F32),
        scratch_shapes=[pltpu.VMEM((hist + rows, pool_w), F32),
                        pltpu.VMEM((rows, d_model), BF16),
                        pltpu.VMEM((rows, d_model), F32)],
        compiler_params=pltpu.CompilerParams(
            dimension_semantics=("arbitrary",), vmem_limit_bytes=VMEM_LIMIT),
        name="merge_ffn",
    )(u, u, u_meta, x2d, o, gp, ga, *weights)


def _t5_bucket(rel):
    nb = N_BUCKETS // 2
    ret = jnp.where(rel > 0, nb, 0)
    n = jnp.abs(rel)
    max_exact = nb // 2
    nf = jnp.maximum(n, max_exact).astype(jnp.float32)
    large = max_exact + (jnp.log(nf / max_exact) / math.log(MAX_DISTANCE / max_exact)
                         * (nb - max_exact)).astype(jnp.int32)
    large = jnp.minimum(large, nb - 1)
    return ret + jnp.where(n < max_exact, n, large)


def _bias_kernel(w_ref, diag_ref, meta_ref):
    _, tk, tq = diag_ref.shape
    n_meta = meta_ref.shape[1]
    period = w_ref.shape[1]

    def toeplitz(row, n_rows):
        x = jnp.broadcast_to(w_ref[row:row + 1, :], (n_rows, period))
        return pltpu.roll(x, 0, 1, stride=1, stride_axis=0)[:, :tq]

    key = lax.broadcasted_iota(jnp.int32, (tk, tq), 0)
    qry = lax.broadcasted_iota(jnp.int32, (tk, tq), 1)
    diag_ref[0] = jnp.where(key // CHUNK <= qry // CHUNK, toeplitz(0, tk), NEG_INF)
    diag_ref[1] = toeplitz(1, tk)
    meta_ref[0] = toeplitz(2, n_meta)
    meta_ref[1] = jnp.broadcast_to(w_ref[3:4, :tq], (n_meta, tq))


def _bias_tables(rel_bias_table):
    tq, tk = ATTN_Q, ATTN_K
    n_heads = rel_bias_table.shape[1]
    assert tq == tk and tk % CHUNK == 0 and tk + 1 >= MAX_DISTANCE
    period = 2 * tk
    signed = np.where(np.arange(period) < tk, np.arange(period), np.arange(period) - period)
    rel = np.stack([
        -signed,
        -signed - tk,
        -signed - N_META,
        np.full(period, -(tk + N_META)),
    ])
    vectors = rel_bias_table[_t5_bucket(jnp.asarray(rel))].astype(F32) * LOG2_E
    vectors = jnp.transpose(vectors, (2, 0, 1))
    diag, meta = pl.pallas_call(
        _bias_kernel,
        grid=(n_heads,),
        in_specs=[pl.BlockSpec((None, 4, period), lambda h: (h, 0, 0))],
        out_specs=(pl.BlockSpec((None, 2, tk, tq), lambda h: (h, 0, 0, 0)),
                   pl.BlockSpec((None, 2, N_META, tq), lambda h: (h, 0, 0, 0))),
        out_shape=(jax.ShapeDtypeStruct((n_heads, 2, tk, tq), F32),
                   jax.ShapeDtypeStruct((n_heads, 2, N_META, tq), F32)),
        compiler_params=pltpu.CompilerParams(dimension_semantics=("arbitrary",)),
        name="bias_tiles",
    )(vectors)
    return diag, meta, vectors[:, 3, 0]


def kernel(x, meta_tokens, rel_bias_table, mix_norm_w, w_in, pool_group_w, pool_scale,
           lambda_q1, lambda_k1, lambda_q2, lambda_k2, subln_w, w_pool_out, w_attn_out,
           w_o, ffn_norm_w, w_gate, w_up, w_down, final_norm_w):
    batch, seq, d_model = x.shape
    assert w_in.shape[0] == 1, "single-layer block"
    pool_w = pool_scale.shape[1]
    attn_w = N_HEADS * V_DIM
    assert seq % ATTN_Q == 0 and seq % MERGE_ROWS == 0 and (batch * seq) % PROJ_ROWS == 0
    assert N_META >= max(POOL_WINDOWS) and MERGE_ROWS % N_META == 0

    w_in_bf = w_in[0].astype(BF16)
    x2d = x.reshape(batch * seq, d_model)
    u, q, k, v, gp, ga = _in_proj(x2d, mix_norm_w[0], w_in_bf, rows=PROJ_ROWS,
                                  pool_w=pool_w, attn_w=attn_w)
    u_meta, _, k_meta, v_meta, _, _ = _in_proj(meta_tokens.astype(x.dtype), mix_norm_w[0], w_in_bf,
                                               rows=N_META, pool_w=pool_w, attn_w=attn_w)
    v_meta_t = v_meta.T.reshape(N_HEADS, V_DIM, N_META)
    ones_rows = jnp.zeros((N_HEADS, BF16_ROWS, N_META), BF16).at[:, 0, :].set(1)
    v_meta_t = jnp.pad(jnp.concatenate([v_meta_t, ones_rows], axis=1),
                       ((0, 0), (0, 0), (0, LANES - N_META))).reshape(N_HEADS * V_ROWS, LANES)

    bias_diag, bias_meta, far_bias = _bias_tables(rel_bias_table)
    lam_init = 0.8 - 0.6 * math.exp(-0.3 * LAYER_IDX)
    lam_vecs = jnp.stack([lambda_q1[0], lambda_k1[0], lambda_q2[0], lambda_k2[0]]).astype(F32)
    o = _diff_attn(q, k, v, k_meta, v_meta_t, bias_diag, bias_meta, far_bias, lam_vecs,
                   subln_w[0].reshape(V_DIM, 1).astype(F32),
                   batch=batch, seq=seq, lam_init=lam_init)

    assert w_gate.shape[2] % FFN_COLS == 0
    weights = (
        pool_group_w[0].astype(BF16), pool_scale[0].reshape(1, pool_w),
        w_pool_out[0].astype(BF16), w_attn_out[0].astype(BF16), w_o[0].astype(BF16),
        ffn_norm_w[0].reshape(1, d_model),
        w_gate[0].astype(BF16), w_up[0].astype(BF16), w_down[0].astype(BF16),
        final_norm_w.reshape(1, d_model))
    out = _merge_ffn(u, u_meta, x2d, o, gp, ga, weights, seq=seq)
    return out.reshape(batch, seq, d_model)
```
